```python
import math
import jax, jax.numpy as jnp
from jax import lax
import numpy as np

D_MODEL = 2048
BATCH = 8
SEQ = 4096
DEPTH = 4
DEC_BATCH = 16
DEC_SEQ = 16
PAST_LEN = 2048

CHUNK = 64
N_MIXERS = 2
N_SSD_LAYERS = (DEPTH + 1) // 2
N_ATT_LAYERS = DEPTH // 2

SSD_EXPAND = 2
D_INNER = SSD_EXPAND * D_MODEL
SSD_HEAD_DIM = 64
SSD_HEADS = D_INNER // SSD_HEAD_DIM
SSD_GROUPS = 8
SSD_HEADS_PER_GROUP = SSD_HEADS // SSD_GROUPS
D_STATE = 128
D_CONV = 4
CONV_DIM = D_INNER + 2 * SSD_GROUPS * D_STATE
SSD_IN_DIM = D_INNER + CONV_DIM + SSD_HEADS
SSD_BLOCK = CHUNK

ATT_HEADS = 8
ATT_HEAD_DIM = D_MODEL // (2 * ATT_HEADS)
ATT_QK_DIM = ATT_HEADS * 2 * ATT_HEAD_DIM
ATT_V_DIM = ATT_HEADS * 2 * ATT_HEAD_DIM
Q_BLOCK = 128

D_FF = 4 * D_MODEL
EPS = 1e-5

kernel_name = 'hybrid_ssd_diffattn_stream_step'


def rmsnorm(x, w):
    xf = x.astype(jnp.float32)
    y = xf * lax.rsqrt(jnp.mean(xf * xf, axis=-1, keepdims=True) + EPS)
    return (y * w.astype(jnp.float32)).astype(x.dtype)


def sqrelu_mlp(x, w_up, w_down):
    h = jax.nn.relu(x @ w_up)
    return (h * h) @ w_down


def causal_conv(xbc, conv_state, w, b):
    xp = jnp.concatenate([conv_state.astype(xbc.dtype), xbc], axis=1)
    y = lax.conv_general_dilated(
        xp, w[:, None, :].astype(xp.dtype), window_strides=(1,), padding='VALID',
        dimension_numbers=('NWC', 'WIO', 'NWC'), feature_group_count=xp.shape[-1])
    return y + b.astype(y.dtype), xp[:, -(D_CONV - 1):]


def ssd_scan(x, dt, a, bm, cm, h0):
    bsz, seq = x.shape[:2]
    blk = min(SSD_BLOCK, seq)
    nc = seq // blk

    def to_blocks(t):
        return jnp.moveaxis(t.reshape((bsz, nc, blk) + t.shape[2:]), 1, 0)

    xs = (to_blocks(x * dt[..., None]), to_blocks(a), to_blocks(bm), to_blocks(cm))
    causal = jnp.tril(jnp.ones((blk, blk), dtype=bool))

    def step(h, inp):
        xdt, ac, bc, cc = inp
        acum = jnp.cumsum(ac, axis=1)
        seg = acum[:, :, None] - acum[:, None, :]
        decay = jnp.exp(jnp.where(causal[None, :, :, None, None], seg, -jnp.inf))
        cb = jnp.einsum('blgn,bsgn->blsg', cc, bc)
        y = jnp.einsum('blsg,blsgr,bsgrp->blgrp', cb, decay, xdt)
        y = y + jnp.einsum('blgn,bgrpn->blgrp', cc, h) * jnp.exp(acum)[..., None]
        tail = jnp.exp(acum[:, -1:] - acum)
        h = h * jnp.exp(acum[:, -1])[..., None, None] + jnp.einsum(
            'blgn,blgrp->bgrpn', bc, xdt * tail[..., None])
        return h, y

    h, ys = lax.scan(step, h0, xs)
    return jnp.moveaxis(ys, 0, 1).reshape(x.shape), h


def ssd_mixer(x, conv_state, ssm_state, w_in, conv_w, conv_b, dt_bias, a_log, d_skip, norm_w, w_out):
    f32 = jnp.float32
    bsz, seq, _ = x.shape
    zxbcdt = x @ w_in
    z, xbc, dt = jnp.split(zxbcdt, [D_INNER, D_INNER + CONV_DIM], axis=-1)
    xbc, new_conv = causal_conv(xbc, conv_state, conv_w, conv_b)
    xbc = jax.nn.silu(xbc)
    xs, bm, cm = jnp.split(xbc, [D_INNER, D_INNER + SSD_GROUPS * D_STATE], axis=-1)
    xs = xs.astype(f32).reshape(bsz, seq, SSD_GROUPS, SSD_HEADS_PER_GROUP, SSD_HEAD_DIM)
    bm = bm.astype(f32).reshape(bsz, seq, SSD_GROUPS, D_STATE)
    cm = cm.astype(f32).reshape(bsz, seq, SSD_GROUPS, D_STATE)
    dt = jax.nn.softplus(dt.astype(f32) + dt_bias.astype(f32))
    dt = dt.reshape(bsz, seq, SSD_GROUPS, SSD_HEADS_PER_GROUP)
    a_neg = -jnp.exp(a_log.astype(f32)).reshape(SSD_GROUPS, SSD_HEADS_PER_GROUP)
    h0 = ssm_state.astype(f32).reshape(bsz, SSD_GROUPS, SSD_HEADS_PER_GROUP, SSD_HEAD_DIM, D_STATE)
    y, h = ssd_scan(xs, dt, dt * a_neg, bm, cm, h0)
    y = y + xs * d_skip.astype(f32).reshape(SSD_GROUPS, SSD_HEADS_PER_GROUP, 1)
    g = (y.reshape(bsz, seq, D_INNER) * jax.nn.silu(z.astype(f32)))
    g = g.reshape(bsz, seq, SSD_GROUPS, D_INNER // SSD_GROUPS)
    g = g * lax.rsqrt(jnp.mean(g * g, axis=-1, keepdims=True) + EPS)
    g = g.reshape(bsz, seq, D_INNER) * norm_w.astype(f32)
    out = g.astype(x.dtype) @ w_out
    return out, new_conv, h.reshape(bsz, SSD_HEADS, SSD_HEAD_DIM, D_STATE)


def diff_attend(q, k, v, q_pos, k_pos, lam):
    f32 = jnp.float32
    s = jnp.einsum('bqhmd,bkhmd->bhmqk', q.astype(f32), k.astype(f32)) * (ATT_HEAD_DIM ** -0.5)
    slopes = 2.0 ** (-8.0 * jnp.arange(1, ATT_HEADS + 1, dtype=f32) / ATT_HEADS)
    dist = jnp.abs(q_pos[:, None] - k_pos[None, :]).astype(f32)
    visible = (k_pos[None, :] // CHUNK) <= (q_pos[:, None] // CHUNK)
    s = s - slopes[:, None, None, None] * dist
    s = jnp.where(visible, s, -jnp.inf)
    p = jax.nn.softmax(s, axis=-1)
    w = p[:, :, 0] - lam * p[:, :, 1]
    return jnp.einsum('bhqk,bkhe->bqhe', w, v.astype(f32))


def diff_attn_mixer(x, cache_k, cache_v, w_qkv, lam_q, lam_k, subln_w, w_o, layer_idx):
    f32 = jnp.float32
    bsz, seq, _ = x.shape
    qkv = x @ w_qkv
    q, k, v = jnp.split(qkv, [ATT_QK_DIM, 2 * ATT_QK_DIM], axis=-1)
    q = q.reshape(bsz, seq, ATT_HEADS, 2, ATT_HEAD_DIM)
    k = k.reshape(bsz, seq, ATT_HEADS, 2, ATT_HEAD_DIM)
    v = v.reshape(bsz, seq, ATT_HEADS, 2 * ATT_HEAD_DIM)
    lam_init = 0.8 - 0.6 * math.exp(-0.3 * layer_idx)
    lq = lam_q.astype(f32)
    lk = lam_k.astype(f32)
    lam = jnp.exp(jnp.sum(lq[0] * lk[0])) - jnp.exp(jnp.sum(lq[1] * lk[1])) + lam_init
    if cache_k is None:
        nb = seq // Q_BLOCK
        qb = jnp.moveaxis(q.reshape(bsz, nb, Q_BLOCK, ATT_HEADS, 2, ATT_HEAD_DIM), 1, 0)
        starts = jnp.arange(nb, dtype=jnp.int32) * Q_BLOCK
        k_pos = jnp.arange(seq, dtype=jnp.int32)

        def one_block(args):
            q_blk, st = args
            q_pos = st + jnp.arange(Q_BLOCK, dtype=jnp.int32)
            return diff_attend(q_blk, k, v, q_pos, k_pos, lam)

        o = lax.map(one_block, (qb, starts))
        o = jnp.moveaxis(o, 0, 1).reshape(bsz, seq, ATT_HEADS, 2 * ATT_HEAD_DIM)
    else:
        past = cache_k.shape[1]
        k_all = jnp.concatenate(
            [cache_k.astype(k.dtype).reshape(bsz, past, ATT_HEADS, 2, ATT_HEAD_DIM), k], axis=1)
        v_all = jnp.concatenate([cache_v.astype(v.dtype), v], axis=1)
        q_pos = past + jnp.arange(seq, dtype=jnp.int32)
        k_pos = jnp.arange(past + seq, dtype=jnp.int32)
        o = diff_attend(q, k_all, v_all, q_pos, k_pos, lam)
    o = rmsnorm(o, subln_w) * (1.0 - lam_init)
    out = o.reshape(bsz, seq, ATT_V_DIM).astype(x.dtype) @ w_o
    return out, k.reshape(bsz, seq, ATT_HEADS, 2 * ATT_HEAD_DIM), v


def setup_inputs(seed: int = 0) -> dict:
    key = jax.random.key(seed)
    ks = jax.random.split(key, 24)
    f32 = jnp.float32

    def nrm(k, shape, scale):
        return jax.random.normal(k, shape, f32) * scale

    def gain(k, shape):
        return 1.0 + 0.02 * jax.random.normal(k, shape, f32)

    dt0 = jnp.exp(jax.random.uniform(ks[11], (N_SSD_LAYERS, SSD_HEADS), f32,
                                     math.log(1e-3), math.log(1e-1)))
    dt_bias = dt0 + jnp.log(-jnp.expm1(-dt0))
    a_log = jnp.log(jax.random.uniform(ks[12], (N_SSD_LAYERS, SSD_HEADS), f32, 1.0, 16.0))
    return {
        'x_prompt': nrm(ks[0], (BATCH, SEQ, D_MODEL), 1.0),
        'x_sample': nrm(ks[1], (DEC_BATCH, DEC_SEQ, D_MODEL), 1.0),
        'cache_k': nrm(ks[2], (N_ATT_LAYERS, DEC_BATCH, PAST_LEN, ATT_HEADS, 2 * ATT_HEAD_DIM), 1.0),
        'cache_v': nrm(ks[3], (N_ATT_LAYERS, DEC_BATCH, PAST_LEN, ATT_HEADS, 2 * ATT_HEAD_DIM), 1.0),
        'state_ssm': nrm(ks[4], (N_SSD_LAYERS, DEC_BATCH, SSD_HEADS, SSD_HEAD_DIM, D_STATE), 0.1),
        'state_conv': nrm(ks[5], (N_SSD_LAYERS, DEC_BATCH, D_CONV - 1, CONV_DIM), 1.0),
        'norm_mix_w': gain(ks[6], (DEPTH, D_MODEL)),
        'norm_mlp_w': gain(ks[7], (DEPTH, D_MODEL)),
        'final_norm_w': gain(ks[8], (D_MODEL,)),
        'ssd_w_in': nrm(ks[9], (N_SSD_LAYERS, D_MODEL, SSD_IN_DIM), D_MODEL ** -0.5),
        'ssd_conv_w': nrm(ks[10], (N_SSD_LAYERS, D_CONV, CONV_DIM), D_CONV ** -0.5),
        'ssd_conv_b': nrm(ks[13], (N_SSD_LAYERS, CONV_DIM), 0.02),
        'ssd_dt_bias': dt_bias,
        'ssd_a_log': a_log,
        'ssd_d': gain(ks[14], (N_SSD_LAYERS, SSD_HEADS)),
        'ssd_norm_w': gain(ks[15], (N_SSD_LAYERS, D_INNER)),
        'ssd_w_out': nrm(ks[16], (N_SSD_LAYERS, D_INNER, D_MODEL), D_INNER ** -0.5),
        'att_w_qkv': nrm(ks[17], (N_ATT_LAYERS, D_MODEL, 2 * ATT_QK_DIM + ATT_V_DIM), D_MODEL ** -0.5),
        'att_lam_q': nrm(ks[18], (N_ATT_LAYERS, 2, ATT_HEAD_DIM), 0.1),
        'att_lam_k': nrm(ks[19], (N_ATT_LAYERS, 2, ATT_HEAD_DIM), 0.1),
        'att_subln_w': gain(ks[20], (N_ATT_LAYERS, 2 * ATT_HEAD_DIM)),
        'att_w_o': nrm(ks[21], (N_ATT_LAYERS, ATT_V_DIM, D_MODEL), ATT_V_DIM ** -0.5),
        'mlp_w_up': nrm(ks[22], (DEPTH, D_MODEL, D_FF), D_MODEL ** -0.5),
        'mlp_w_down': nrm(ks[23], (DEPTH, D_FF, D_MODEL), D_FF ** -0.5),
    }


def reference(x_prompt, x_sample, cache_k, cache_v, state_ssm, state_conv,
              norm_mix_w, norm_mlp_w, final_norm_w,
              ssd_w_in, ssd_conv_w, ssd_conv_b, ssd_dt_bias, ssd_a_log, ssd_d, ssd_norm_w, ssd_w_out,
              att_w_qkv, att_lam_q, att_lam_k, att_subln_w, att_w_o,
              mlp_w_up, mlp_w_down):
    hp, hs = x_prompt, x_sample
    bp = x_prompt.shape[0]
    k_new_p, v_new_p, ssm_new_p, conv_new_p = [], [], [], []
    k_new_s, v_new_s, ssm_new_s, conv_new_s = [], [], [], []
    for i in range(DEPTH):
        j = i // N_MIXERS
        xp_n = rmsnorm(hp, norm_mix_w[i])
        xs_n = rmsnorm(hs, norm_mix_w[i])
        if i % N_MIXERS == 0:
            params = (ssd_w_in[j], ssd_conv_w[j], ssd_conv_b[j], ssd_dt_bias[j], ssd_a_log[j],
                      ssd_d[j], ssd_norm_w[j], ssd_w_out[j])
            zero_conv = jnp.zeros((bp, D_CONV - 1, CONV_DIM), hp.dtype)
            zero_ssm = jnp.zeros((bp, SSD_HEADS, SSD_HEAD_DIM, D_STATE), jnp.float32)
            op, cp, sp = ssd_mixer(xp_n, zero_conv, zero_ssm, *params)
            os_, cs, ss = ssd_mixer(xs_n, state_conv[j], state_ssm[j], *params)
            conv_new_p.append(cp)
            ssm_new_p.append(sp)
            conv_new_s.append(cs)
            ssm_new_s.append(ss)
        else:
            params = (att_w_qkv[j], att_lam_q[j], att_lam_k[j], att_subln_w[j], att_w_o[j])
            op, kp, vp = diff_attn_mixer(xp_n, None, None, *params, i)
            os_, ks_, vs_ = diff_attn_mixer(xs_n, cache_k[j], cache_v[j], *params, i)
            k_new_p.append(kp)
            v_new_p.append(vp)
            k_new_s.append(ks_)
            v_new_s.append(vs_)
        hp = hp + op
        hs = hs + os_
        hp = hp + sqrelu_mlp(rmsnorm(hp, norm_mlp_w[i]), mlp_w_up[i], mlp_w_down[i])
        hs = hs + sqrelu_mlp(rmsnorm(hs, norm_mlp_w[i]), mlp_w_up[i], mlp_w_down[i])
    y_prompt = rmsnorm(hp, final_norm_w)
    y_sample = rmsnorm(hs, final_norm_w)
    return (y_prompt, y_sample,
            jnp.stack(k_new_p), jnp.stack(v_new_p), jnp.stack(ssm_new_p), jnp.stack(conv_new_p),
            jnp.stack(k_new_s), jnp.stack(v_new_s), jnp.stack(ssm_new_s), jnp.stack(conv_new_s))
```

```python
import functools
import math

import numpy as np
import jax
import jax.numpy as jnp
from jax import lax
from jax.experimental import pallas as pl
from jax.experimental.pallas import tpu as pltpu

F32 = jnp.float32
BF16 = jnp.bfloat16
HIGHEST = lax.Precision.HIGHEST

EPS = 1e-5
CHUNK = 64
CHUNK_SHIFT = 6
D_CONV = 4
SSD_HEAD_DIM = 64
SSD_GROUPS = 8
D_STATE = 128
ATT_HEADS = 8
NEG = -1e30

V7X_LANES = 128
V7X_SUBLANES = 8
V7X_VMEM_LIMIT_BYTES = 56 * 1024 * 1024

_NT = (((1,), (1,)), ((), ()))


def _cparams(n_axes):
    return pltpu.CompilerParams(dimension_semantics=("arbitrary",) * n_axes,
                                vmem_limit_bytes=V7X_VMEM_LIMIT_BYTES)


def _silu(u):
    return u * (1.0 / (1.0 + jnp.exp(-u)))


def _softplus(u):
    return jnp.maximum(u, 0.0) + jnp.log1p(jnp.exp(-jnp.abs(u)))


NORM_ROWS = 16


def _norm_mm_kernel(x_ref, nw_ref, w_ref, *rest, seg_lo, seg_hi, seg_width, seg_scale, relu2):
    out_refs, xn_ref = rest[:-1], rest[-1]
    j = pl.program_id(1)
    tm = x_ref.shape[0]
    tn = w_ref.shape[1]

    @pl.when(j == 0)
    def _normalise():
        nw = nw_ref[...]

        def body(c, carry):
            r0 = pl.multiple_of(c * NORM_ROWS, NORM_ROWS)
            x = x_ref[pl.ds(r0, NORM_ROWS), :]
            ms = jnp.mean(x * x, axis=-1, keepdims=True)
            xn_ref[pl.ds(r0, NORM_ROWS), :] = (x * lax.rsqrt(ms + EPS) * nw).astype(BF16)
            return carry

        lax.fori_loop(0, tm // NORM_ROWS, body, 0)

    acc = jnp.dot(xn_ref[...], w_ref[...], preferred_element_type=F32)
    if relu2:
        acc = jnp.square(jnp.maximum(acc, 0.0))

    for s, o_ref in enumerate(out_refs):
        def write(o_ref=o_ref, s=s):
            r = acc if seg_width[s] == tn else acc[:, :seg_width[s]]
            if seg_scale[s] != 1.0:
                r = r * seg_scale[s]
            o_ref[...] = r.astype(o_ref.dtype)

        if len(out_refs) == 1:
            write()
        else:
            pl.when((j >= seg_lo[s]) & (j < seg_hi[s]))(write)


def _norm_mm(x, nw, w, segs, *, tn, relu2=False):
    m, kdim = x.shape
    n = w.shape[1]
    tm = min(m, 1024)
    assert m % tm == 0 and n % tn == 0 and tm % NORM_ROWS == 0
    lo, seg_lo, seg_hi = 0, [], []
    for nt, _, _, _ in segs:
        seg_lo.append(lo)
        lo += nt
        seg_hi.append(lo)
    assert lo == n // tn
    out_shape, out_specs = [], []
    for (nt, width, dtype, _), s_lo in zip(segs, seg_lo):
        out_shape.append(jax.ShapeDtypeStruct((m, nt * width), dtype))
        out_specs.append(pl.BlockSpec(
            (tm, width), lambda i, j, s_lo=s_lo, nt=nt: (i, jnp.clip(j - s_lo, 0, nt - 1))))
    kern = functools.partial(
        _norm_mm_kernel, seg_lo=tuple(seg_lo), seg_hi=tuple(seg_hi),
        seg_width=tuple(s[1] for s in segs), seg_scale=tuple(float(s[3]) for s in segs), relu2=relu2)
    return pl.pallas_call(
        kern,
        grid=(m // tm, n // tn),
        in_specs=[pl.BlockSpec((tm, kdim), lambda i, j: (i, 0)),
                  pl.BlockSpec((1, kdim), lambda i, j: (0, 0)),
                  pl.BlockSpec((kdim, tn), lambda i, j: (0, j))],
        out_specs=out_specs,
        out_shape=out_shape,
        scratch_shapes=[pltpu.VMEM((tm, kdim), BF16)],
        compiler_params=_cparams(2),
        name="norm_mm",
    )(x, nw.reshape(1, kdim), w)


def _mm_res_kernel(a_ref, w_ref, r_ref, o_ref):
    o_ref[...] = r_ref[...] + jnp.dot(a_ref[...], w_ref[...], preferred_element_type=F32)


def _mm_res(a, w, res, *, tn=512):
    m, kdim = a.shape
    n = w.shape[1]
    tm = min(m, 512)
    assert m % tm == 0 and n % tn == 0
    return pl.pallas_call(
        _mm_res_kernel,
        grid=(m // tm, n // tn),
        in_specs=[pl.BlockSpec((tm, kdim), lambda i, j: (i, 0)),
                  pl.BlockSpec((kdim, tn), lambda i, j: (0, j)),
                  pl.BlockSpec((tm, tn), lambda i, j: (i, j))],
        out_specs=pl.BlockSpec((tm, tn), lambda i, j: (i, j)),
        out_shape=jax.ShapeDtypeStruct((m, n), F32),
        compiler_params=_cparams(2),
        name="mm_res",
    )(a, w, res)


def _rmsnorm_kernel(x_ref, nw_ref, o_ref):
    nw = nw_ref[...]

    def body(c, carry):
        r0 = pl.multiple_of(c * NORM_ROWS, NORM_ROWS)
        x = x_ref[pl.ds(r0, NORM_ROWS), :]
        ms = jnp.mean(x * x, axis=-1, keepdims=True)
        o_ref[pl.ds(r0, NORM_ROWS), :] = x * lax.rsqrt(ms + EPS) * nw
        return carry

    lax.fori_loop(0, x_ref.shape[0] // NORM_ROWS, body, 0)


def _rmsnorm(x, nw):
    m, d = x.shape
    tm = min(m, 512)
    return pl.pallas_call(
        _rmsnorm_kernel,
        grid=(m // tm,),
        in_specs=[pl.BlockSpec((tm, d), lambda i: (i, 0)), pl.BlockSpec((1, d), lambda i: (0, 0))],
        out_specs=pl.BlockSpec((tm, d), lambda i: (i, 0)),
        out_shape=jax.ShapeDtypeStruct((m, d), F32),
        compiler_params=_cparams(1),
        name="final_rmsnorm",
    )(x, nw.reshape(1, d))


CONV_SLAB = 512
CONV_ROWS = 32
HALO = V7X_SUBLANES


def _ssd_kernel(z_ref, xbc_ref, dt_ref, conv0_ref, h0_ref, cw_ref, cb_ref, dtb_ref, alog_ref,
                dsk_ref, nw_ref, e_ref, g_ref, convo_ref, ho_ref, xp_ref, ht_ref, xs_ref, bc_ref):
    t = pl.program_id(1)
    nt = pl.num_programs(1)
    L = z_ref.shape[0]
    d_inner = z_ref.shape[1]
    n_bc = bc_ref.shape[1]
    gw = d_inner // SSD_GROUPS
    hpg = gw // SSD_HEAD_DIM
    rows = min(L, CONV_ROWS)

    eye_r = lax.broadcasted_iota(jnp.int32, (V7X_LANES, V7X_LANES), 0)
    eye_c = lax.broadcasted_iota(jnp.int32, (V7X_LANES, V7X_LANES), 1)
    eye_f = (eye_r == eye_c).astype(F32)
    eye_b = eye_f.astype(BF16)

    @pl.when(t == 0)
    def _init():
        xp_ref[0:HALO, :] = conv0_ref[...]
        for g in range(SSD_GROUPS):
            ht_ref[g] = jnp.transpose(h0_ref[g])

    xp_ref[HALO:HALO + L, :] = xbc_ref[...].astype(F32)

    def conv_rows(c0, r0):
        u = cb_ref[:, pl.ds(c0, CONV_SLAB)]
        for k in range(D_CONV):
            tap = xp_ref[pl.ds(HALO - (D_CONV - 1) + k + r0, rows), pl.ds(c0, CONV_SLAB)]
            u = u + cw_ref[pl.ds(k, 1), pl.ds(c0, CONV_SLAB)] * tap
        return _silu(u)

    def x_body(c, carry):
        c0 = pl.multiple_of(c * CONV_SLAB, CONV_SLAB)
        for r0 in range(0, L, rows):
            xs_ref[pl.ds(r0, rows), pl.ds(c0, CONV_SLAB)] = conv_rows(c0, r0)
        return carry

    lax.fori_loop(0, d_inner // CONV_SLAB, x_body, 0)

    def bc_body(c, carry):
        c0 = pl.multiple_of(c * CONV_SLAB, CONV_SLAB)
        for r0 in range(0, L, rows):
            bc_ref[pl.ds(r0, rows), pl.ds(c0, CONV_SLAB)] = conv_rows(d_inner + c0, r0).astype(BF16)
        return carry

    lax.fori_loop(0, n_bc // CONV_SLAB, bc_body, 0)

    dt = _softplus(dt_ref[...] + dtb_ref[...])
    a = dt * (-jnp.exp(alog_ref[...]))
    ri = lax.broadcasted_iota(jnp.int32, (L, L), 0)
    ci = lax.broadcasted_iota(jnp.int32, (L, L), 1)
    tri = ri >= ci
    acum = jnp.dot(tri.astype(F32), a, precision=HIGHEST, preferred_element_type=F32)
    acum_t = lax.dot_general(eye_f, acum, _NT, precision=HIGHEST, preferred_element_type=F32)
    last = acum[L - 1:L, :]
    dt_b = dt.astype(BF16)
    ea_b = jnp.exp(acum).astype(BF16)
    tail_b = jnp.exp(last - acum).astype(BF16)
    cd8 = jnp.broadcast_to(jnp.exp(last), (V7X_SUBLANES, V7X_LANES))
    lane = lax.broadcasted_iota(jnp.int32, (L, V7X_LANES), 1)
    lane_lo = lane < SSD_HEAD_DIM

    for g in range(SSD_GROUPS):
        cs = slice(g * gw, (g + 1) * gw)
        e_g = e_ref[:, cs]
        dtx = jnp.dot(dt_b, e_g, preferred_element_type=F32)
        tlx = jnp.dot(tail_b, e_g, preferred_element_type=F32)
        eax = jnp.dot(ea_b, e_g, preferred_element_type=F32)
        cdx = jnp.dot(cd8, e_g.astype(F32), precision=HIGHEST, preferred_element_type=F32)[0:1, :]
        xs_g = xs_ref[:, cs]
        xdt = xs_g * dtx
        b_g = bc_ref[:, g * D_STATE:(g + 1) * D_STATE]
        c_g = bc_ref[:, n_bc // 2 + g * D_STATE:n_bc // 2 + (g + 1) * D_STATE]
        cb = lax.dot_general(c_g, b_g, _NT, preferred_element_type=F32)
        ht = ht_ref[g]
        y_state = jnp.dot(c_g, ht.astype(BF16), preferred_element_type=F32) * eax
        b_gt = lax.dot_general(eye_b, b_g, _NT, preferred_element_type=F32).astype(BF16)
        ht_ref[g] = ht * cdx + jnp.dot(b_gt, (xdt * tlx).astype(BF16), preferred_element_type=F32)

        pairs = []
        for p in range(gw // V7X_LANES):
            ls = slice(p * V7X_LANES, (p + 1) * V7X_LANES)
            xdt_p = xdt[:, ls]
            y_p = y_state[:, ls] + xs_g[:, ls] * dsk_ref[:, g * gw + p * V7X_LANES:g * gw + (p + 1) * V7X_LANES]
            for q in range(V7X_LANES // SSD_HEAD_DIM):
                h = g * hpg + p * (V7X_LANES // SSD_HEAD_DIM) + q
                seg = acum[:, h:h + 1] - acum_t[h:h + 1, :]
                m = (jnp.exp(jnp.where(tri, seg, NEG)) * cb).astype(BF16)
                rhs = jnp.where(lane_lo if q == 0 else jnp.logical_not(lane_lo), xdt_p, 0.0).astype(BF16)
                y_p = y_p + jnp.dot(m, rhs, preferred_element_type=F32)
            pairs.append(y_p)
        y_g = jnp.concatenate(pairs, axis=1)
        gate = y_g * _silu(z_ref[:, cs].astype(F32))
        ms = jnp.mean(gate * gate, axis=-1, keepdims=True)
        g_ref[:, cs] = (gate * lax.rsqrt(ms + EPS) * nw_ref[:, cs]).astype(g_ref.dtype)

    xp_ref[0:HALO, :] = xp_ref[L:L + HALO, :]

    @pl.when(t == nt - 1)
    def _finish():
        convo_ref[...] = xp_ref[HALO - (D_CONV - 1):HALO, :]
        for g in range(SSD_GROUPS):
            ho_ref[g] = jnp.transpose(ht_ref[g])


def _ssd_scan(z, xbc, dt, conv0, h0, conv_w, conv_b, dt_bias, a_log, d_skip, norm_w, *, chunk):
    bsz, seq, d_inner = z.shape
    conv_dim = xbc.shape[2]
    n_heads = d_inner // SSD_HEAD_DIM
    gw = d_inner // SSD_GROUPS
    L = min(chunk, seq)
    assert seq % L == 0 and n_heads <= V7X_LANES and L >= HALO
    pad = V7X_LANES - n_heads
    expand = np.zeros((V7X_LANES, d_inner), np.float32)
    expand[np.arange(d_inner) // SSD_HEAD_DIM, np.arange(d_inner)] = 1.0
    conv0_p = jnp.pad(conv0.astype(F32), ((0, 0), (HALO - (D_CONV - 1), 0), (0, 0)))
    row = lambda v: v.astype(F32).reshape(1, -1)
    full = lambda shape: pl.BlockSpec(shape, lambda b, t: (0,) * len(shape))
    g_out, conv_out, h_out = pl.pallas_call(
        _ssd_kernel,
        grid=(bsz, seq // L),
        in_specs=[pl.BlockSpec((None, L, d_inner), lambda b, t: (b, t, 0)),
                  pl.BlockSpec((None, L, conv_dim), lambda b, t: (b, t, 0)),
                  pl.BlockSpec((None, L, V7X_LANES), lambda b, t: (b, t, 0)),
                  pl.BlockSpec((None, HALO, conv_dim), lambda b, t: (b, 0, 0)),
                  pl.BlockSpec((None, SSD_GROUPS, gw, D_STATE), lambda b, t: (b, 0, 0, 0)),
                  full((D_CONV, conv_dim)), full((1, conv_dim)), full((1, V7X_LANES)), full((1, V7X_LANES)),
                  full((1, d_inner)), full((1, d_inner)), full((V7X_LANES, d_inner))],
        out_specs=[pl.BlockSpec((None, L, d_inner), lambda b, t: (b, t, 0)),
                   pl.BlockSpec((None, D_CONV - 1, conv_dim), lambda b, t: (b, 0, 0)),
                   pl.BlockSpec((None, SSD_GROUPS, gw, D_STATE), lambda b, t: (b, 0, 0, 0))],
        out_shape=[jax.ShapeDtypeStruct((bsz, seq, d_inner), BF16),
                   jax.ShapeDtypeStruct((bsz, D_CONV - 1, conv_dim), F32),
                   jax.ShapeDtypeStruct((bsz, SSD_GROUPS, gw, D_STATE), F32)],
        scratch_shapes=[pltpu.VMEM((L + HALO, conv_dim), F32),
                        pltpu.VMEM((SSD_GROUPS, D_STATE, gw), F32),
                        pltpu.VMEM((L, d_inner), F32),
                        pltpu.VMEM((L, conv_dim - d_inner), BF16)],
        compiler_params=_cparams(2),
        name="ssd_scan",
    )(z, xbc, dt, conv0_p, h0.astype(F32).reshape(bsz, SSD_GROUPS, gw, D_STATE),
      conv_w.astype(F32), row(conv_b), jnp.pad(row(dt_bias), ((0, 0), (0, pad))),
      jnp.pad(row(a_log), ((0, 0), (0, pad))), row(jnp.repeat(d_skip, SSD_HEAD_DIM)), row(norm_w),
      jnp.asarray(expand, BF16))
    return g_out, conv_out, h_out.reshape(bsz, n_heads, SSD_HEAD_DIM, D_STATE)


def _diff_finalize(o1, o2, lq_ref, lk_ref, sw_ref, lam_init):
    e = jnp.exp(jnp.sum(lq_ref[...] * lk_ref[...], axis=-1, keepdims=True))
    lam = e[0:1, :] - e[1:2, :] + lam_init
    o = o1 - lam * o2
    ms = jnp.mean(o * o, axis=-1, keepdims=True)
    return o * lax.rsqrt(ms + EPS) * sw_ref[...] * (1.0 - lam_init)


def _flash_kernel(slope_ref, q_ref, k_ref, v_ref, lq_ref, lk_ref, sw_ref, o_ref,
                  kb_ref, vb_ref, m_ref, l_ref, acc_ref, *, lam_init):
    h = pl.program_id(1)
    qi = pl.program_id(2)
    tq = q_ref.shape[0]
    seq = k_ref.shape[0]
    dh = q_ref.shape[1] // 2
    slope = slope_ref[h]

    @pl.when(qi == 0)
    def _cast_kv():
        def body(c, carry):
            r0 = pl.multiple_of(c * tq, tq)
            kb_ref[pl.ds(r0, tq), :] = k_ref[pl.ds(r0, tq), :].astype(BF16)
            vb_ref[pl.ds(r0, tq), :] = v_ref[pl.ds(r0, tq), :].astype(BF16)
            return carry
        lax.fori_loop(0, seq // tq, body, 0)

    m_ref[...] = jnp.full(m_ref.shape, NEG, F32)
    l_ref[...] = jnp.zeros(l_ref.shape, F32)
    acc_ref[...] = jnp.zeros(acc_ref.shape, F32)

    q = q_ref[...]
    ri = lax.broadcasted_iota(jnp.int32, (tq, tq), 0)
    ci = lax.broadcasted_iota(jnp.int32, (tq, tq), 1)
    rel = (ri - ci).astype(F32)

    def update(mi, s, vblk):
        m_prev = m_ref[mi]
        m_new = jnp.maximum(m_prev, jnp.max(s, axis=-1, keepdims=True))
        p = jnp.exp(s - m_new)
        alpha = jnp.exp(m_prev - m_new)
        l_ref[mi] = alpha * l_ref[mi] + jnp.sum(p, axis=-1, keepdims=True)
        acc_ref[mi] = alpha * acc_ref[mi] + jnp.dot(p.astype(BF16), vblk, preferred_element_type=F32)
        m_ref[mi] = m_new

    def body(kj, carry):
        r0 = pl.multiple_of(kj * tq, tq)
        kblk = kb_ref[pl.ds(r0, tq), :]
        vblk = vb_ref[pl.ds(r0, tq), :]
        bias = slope * (rel + ((qi - kj) * tq).astype(F32))
        for mi in range(2):
            s = lax.dot_general(q[:, mi * dh:(mi + 1) * dh], kblk[:, mi * dh:(mi + 1) * dh], _NT,
                                preferred_element_type=F32) - bias
            update(mi, s, vblk)
        return carry

    lax.fori_loop(0, qi, body, 0)

    r0 = pl.multiple_of(qi * tq, tq)
    kblk = kb_ref[pl.ds(r0, tq), :]
    vblk = vb_ref[pl.ds(r0, tq), :]
    visible = (ci >> CHUNK_SHIFT) <= (ri >> CHUNK_SHIFT)
    bias = slope * jnp.abs(rel)
    for mi in range(2):
        s = lax.dot_general(q[:, mi * dh:(mi + 1) * dh], kblk[:, mi * dh:(mi + 1) * dh], _NT,
                            preferred_element_type=F32) - bias
        update(mi, jnp.where(visible, s, NEG), vblk)

    o1 = acc_ref[0] / l_ref[0]
    o2 = acc_ref[1] / l_ref[1]
    o_ref[...] = _diff_finalize(o1, o2, lq_ref, lk_ref, sw_ref, lam_init).astype(o_ref.dtype)


def _alibi_slopes():
    return jnp.asarray(2.0 ** (-8.0 * np.arange(1, ATT_HEADS + 1, dtype=np.float32) / ATT_HEADS), F32)


def _flash_attention(q, k, v, lam_q, lam_k, subln_w, lam_init, *, tq=256):
    bsz, seq, width = q.shape
    hw = width // ATT_HEADS
    tq = min(tq, seq)
    assert seq % tq == 0 and tq % CHUNK == 0
    smem = pl.BlockSpec(memory_space=pltpu.SMEM)
    full = lambda shape: pl.BlockSpec(shape, lambda b, h, i: (0,) * len(shape))
    return pl.pallas_call(
        functools.partial(_flash_kernel, lam_init=lam_init),
        grid=(bsz, ATT_HEADS, seq // tq),
        in_specs=[smem,
                  pl.BlockSpec((None, tq, hw), lambda b, h, i: (b, i, h)),
                  pl.BlockSpec((None, seq, hw), lambda b, h, i: (b, 0, h)),
                  pl.BlockSpec((None, seq, hw), lambda b, h, i: (b, 0, h)),
                  full(lam_q.shape), full(lam_k.shape), full((1, hw))],
        out_specs=pl.BlockSpec((None, tq, hw), lambda b, h, i: (b, i, h)),
        out_shape=jax.ShapeDtypeStruct((bsz, seq, width), BF16),
        scratch_shapes=[pltpu.VMEM((seq, hw), BF16), pltpu.VMEM((seq, hw), BF16),
                        pltpu.VMEM((2, tq, 1), F32), pltpu.VMEM((2, tq, 1), F32),
                        pltpu.VMEM((2, tq, hw), F32)],
        compiler_params=_cparams(3),
        name="flash_diff_attn",
    )(_alibi_slopes(), q, k, v, lam_q.astype(F32), lam_k.astype(F32), subln_w.astype(F32).reshape(1, hw))


def _decode_kernel(slope_ref, q_ref, kn_ref, vn_ref, kc_ref, vc_ref, lq_ref, lk_ref, sw_ref, o_ref,
                   *, lam_init):
    h = pl.program_id(1)
    slope = slope_ref[h]
    tq = q_ref.shape[0]
    past = kc_ref.shape[0]
    dh = q_ref.shape[1] // 2
    q = q_ref[...]
    kc = kc_ref[...].astype(BF16)
    vc = vc_ref[...].astype(BF16)
    kn = kn_ref[...].astype(BF16)
    vn = vn_ref[...].astype(BF16)

    q_pos_c = past + lax.broadcasted_iota(jnp.int32, (tq, past), 0)
    k_pos_c = lax.broadcasted_iota(jnp.int32, (tq, past), 1)
    bias_c = slope * jnp.abs(q_pos_c - k_pos_c).astype(F32)
    vis_c = (k_pos_c >> CHUNK_SHIFT) <= (q_pos_c >> CHUNK_SHIFT)
    q_pos_n = past + lax.broadcasted_iota(jnp.int32, (tq, tq), 0)
    k_pos_n = past + lax.broadcasted_iota(jnp.int32, (tq, tq), 1)
    bias_n = slope * jnp.abs(q_pos_n - k_pos_n).astype(F32)
    vis_n = (k_pos_n >> CHUNK_SHIFT) <= (q_pos_n >> CHUNK_SHIFT)

    outs = []
    for mi in range(2):
        qm = q[:, mi * dh:(mi + 1) * dh]
        sc = lax.dot_general(qm, kc[:, mi * dh:(mi + 1) * dh], _NT, preferred_element_type=F32) - bias_c
        sn = lax.dot_general(qm, kn[:, mi * dh:(mi + 1) * dh], _NT, preferred_element_type=F32) - bias_n
        sc = jnp.where(vis_c, sc, NEG)
        sn = jnp.where(vis_n, sn, NEG)
        mx = jnp.maximum(jnp.max(sc, axis=-1, keepdims=True), jnp.max(sn, axis=-1, keepdims=True))
        pc = jnp.exp(sc - mx)
        pn = jnp.exp(sn - mx)
        den = jnp.sum(pc, axis=-1, keepdims=True) + jnp.sum(pn, axis=-1, keepdims=True)
        num = (jnp.dot(pc.astype(BF16), vc, preferred_element_type=F32)
               + jnp.dot(pn.astype(BF16), vn, preferred_element_type=F32))
        outs.append(num / den)
    o_ref[...] = _diff_finalize(outs[0], outs[1], lq_ref, lk_ref, sw_ref, lam_init).astype(o_ref.dtype)


def _decode_attention(q, k_new, v_new, cache_k, cache_v, layer, lam_q, lam_k, subln_w, lam_init):
    bsz, tq, width = q.shape
    past = cache_k.shape[2]
    hw = width // ATT_HEADS
    smem = pl.BlockSpec(memory_space=pltpu.SMEM)
    full = lambda shape: pl.BlockSpec(shape, lambda b, h: (0,) * len(shape))
    new = pl.BlockSpec((None, tq, hw), lambda b, h: (b, 0, h))
    cache = pl.BlockSpec((None, None, past, hw), lambda b, h: (layer, b, 0, h))
    return pl.pallas_call(
        functools.partial(_decode_kernel, lam_init=lam_init),
        grid=(bsz, ATT_HEADS),
        in_specs=[smem, new, new, new, cache, cache, full(lam_q.shape), full(lam_k.shape), full((1, hw))],
        out_specs=new,
        out_shape=jax.ShapeDtypeStruct((bsz, tq, width), BF16),
        compiler_params=_cparams(2),
        name="decode_diff_attn",
    )(_alibi_slopes(), q, k_new, v_new, cache_k, cache_v, lam_q.astype(F32), lam_k.astype(F32),
      subln_w.astype(F32).reshape(1, hw))


MM_TN = 512
SSD_CHUNK = 128


def _ssd_layer(h, bsz, norm_w, w_in_p, conv0, h0, conv_w, conv_b, dt_bias, a_log, d_skip, gnorm_w, w_out,
               d_inner, conv_dim):
    m = h.shape[0]
    seq = m // bsz
    z, xbc, dt = _norm_mm(
        h, norm_w, w_in_p,
        [(d_inner // MM_TN, MM_TN, BF16, 1.0), (conv_dim // MM_TN, MM_TN, BF16, 1.0), (1, V7X_LANES, F32, 1.0)],
        tn=MM_TN)
    g, conv_new, h_new = _ssd_scan(
        z.reshape(bsz, seq, d_inner), xbc.reshape(bsz, seq, conv_dim), dt.reshape(bsz, seq, V7X_LANES),
        conv0, h0, conv_w, conv_b, dt_bias, a_log, d_skip, gnorm_w, chunk=SSD_CHUNK)
    return _mm_res(g.reshape(m, d_inner), w_out, h), conv_new, h_new


def _qkv(h, norm_w, w_qkv, width):
    dh = width // (2 * ATT_HEADS)
    nt = width // MM_TN
    return _norm_mm(h, norm_w, w_qkv,
                    [(nt, MM_TN, BF16, dh ** -0.5), (nt, MM_TN, F32, 1.0), (nt, MM_TN, F32, 1.0)], tn=MM_TN)


def _mlp(h, norm_w, w_up, w_down):
    d_ff = w_up.shape[1]
    mid = _norm_mm(h, norm_w, w_up, [(d_ff // MM_TN, MM_TN, BF16, 1.0)], tn=MM_TN, relu2=True)[0]
    return _mm_res(mid, w_down, h)


def kernel(x_prompt, x_sample, cache_k, cache_v, state_ssm, state_conv, norm_mix_w, norm_mlp_w, final_norm_w, ssd_w_in, ssd_conv_w, ssd_conv_b, ssd_dt_bias, ssd_a_log, ssd_d, ssd_norm_w, ssd_w_out, att_w_qkv, att_lam_q, att_lam_k, att_subln_w, att_w_o, mlp_w_up, mlp_w_down):
    bp, sp, d_model = x_prompt.shape
    bs, ss, _ = x_sample.shape
    depth = norm_mix_w.shape[0]
    d_inner = ssd_w_out.shape[1]
    conv_dim = ssd_conv_w.shape[2]
    n_heads = ssd_d.shape[1]
    att_width = att_w_o.shape[1]
    past = cache_k.shape[2]

    hp = x_prompt.reshape(bp * sp, d_model)
    hs = x_sample.reshape(bs * ss, d_model)
    cache_k = cache_k.reshape(cache_k.shape[0], bs, past, att_width)
    cache_v = cache_v.reshape(cache_v.shape[0], bs, past, att_width)

    k_p, v_p, ssm_p, conv_p, k_s, v_s, ssm_s, conv_s = [], [], [], [], [], [], [], []
    for i in range(depth):
        j = i // 2
        if i % 2 == 0:
            w_in = ssd_w_in[j]
            w_in_p = jnp.concatenate(
                [w_in[:, :d_inner + conv_dim],
                 jnp.pad(w_in[:, d_inner + conv_dim:], ((0, 0), (0, MM_TN - n_heads)))], axis=1).astype(BF16)
            w_out = ssd_w_out[j].astype(BF16)
            args = (ssd_conv_w[j], ssd_conv_b[j], ssd_dt_bias[j], ssd_a_log[j], ssd_d[j], ssd_norm_w[j], w_out,
                    d_inner, conv_dim)
            hp, cp, sp_state = _ssd_layer(
                hp, bp, norm_mix_w[i], w_in_p, jnp.zeros((bp, D_CONV - 1, conv_dim), F32),
                jnp.zeros((bp, n_heads, SSD_HEAD_DIM, D_STATE), F32), *args)
            hs, cs, ss_state = _ssd_layer(hs, bs, norm_mix_w[i], w_in_p, state_conv[j], state_ssm[j], *args)
            conv_p.append(cp)
            ssm_p.append(sp_state)
            conv_s.append(cs)
            ssm_s.append(ss_state)
        else:
            w_qkv = att_w_qkv[j].astype(BF16)
            w_o = att_w_o[j].astype(BF16)
            lam_init = 0.8 - 0.6 * math.exp(-0.3 * i)
            qp, kp, vp = _qkv(hp, norm_mix_w[i], w_qkv, att_width)
            op = _flash_attention(qp.reshape(bp, sp, att_width), kp.reshape(bp, sp, att_width),
                                  vp.reshape(bp, sp, att_width), att_lam_q[j], att_lam_k[j], att_subln_w[j], lam_init)
            hp = _mm_res(op.reshape(bp * sp, att_width), w_o, hp)
            qs, ks, vs = _qkv(hs, norm_mix_w[i], w_qkv, att_width)
            os_ = _decode_attention(qs.reshape(bs, ss, att_width), ks.reshape(bs, ss, att_width),
                                    vs.reshape(bs, ss, att_width), cache_k, cache_v, j,
                                    att_lam_q[j], att_lam_k[j], att_subln_w[j], lam_init)
            hs = _mm_res(os_.reshape(bs * ss, att_width), w_o, hs)
            hw = att_width // ATT_HEADS
            k_p.append(kp.reshape(bp, sp, ATT_HEADS, hw))
            v_p.append(vp.reshape(bp, sp, ATT_HEADS, hw))
            k_s.append(ks.reshape(bs, ss, ATT_HEADS, hw))
            v_s.append(vs.reshape(bs, ss, ATT_HEADS, hw))
        w_up = mlp_w_up[i].astype(BF16)
        w_down = mlp_w_down[i].astype(BF16)
        hp = _mlp(hp, norm_mlp_w[i], w_up, w_down)
        hs = _mlp(hs, norm_mlp_w[i], w_up, w_down)

    y_prompt = _rmsnorm(hp, final_norm_w).reshape(bp, sp, d_model)
    y_sample = _rmsnorm(hs, final_norm_w).reshape(bs, ss, d_model)
    return (y_prompt, y_sample, jnp.stack(k_p), jnp.stack(v_p), jnp.stack(ssm_p), jnp.stack(conv_p),
            jnp.stack(k_s), jnp.stack(v_s), jnp.stack(ssm_s), jnp.stack(conv_s))
```

```python
import functools
import math

import numpy as np
import jax
import jax.numpy as jnp
from jax import lax
from jax.experimental import pallas as pl
from jax.experimental.pallas import tpu as pltpu

F32 = jnp.float32
BF16 = jnp.bfloat16

EPS = 1e-5
CHUNK = 64
CHUNK_SHIFT = 6
D_CONV = 4
SSD_HEAD_DIM = 64
SSD_GROUPS = 8
D_STATE = 128
ATT_HEADS = 8
NEG = -1e30
LOG2E = math.log2(math.e)

V7X_LANES = 128
V7X_SUBLANES = 8
V7X_BF16_ROWS = 16
V7X_VMEM_LIMIT_BYTES = 56 * 1024 * 1024

_NT = (((1,), (1,)), ((), ()))


def _cparams(n_axes):
    return pltpu.CompilerParams(dimension_semantics=("arbitrary",) * n_axes,
                                vmem_limit_bytes=V7X_VMEM_LIMIT_BYTES)


def _silu(u):
    return u * (1.0 / (1.0 + jnp.exp(-u)))


def _softplus(u):
    return jnp.maximum(u, 0.0) + jnp.log1p(jnp.exp(-jnp.abs(u)))


def _split3(x):
    hi = x.astype(BF16)
    r = x - hi.astype(F32)
    mid = r.astype(BF16)
    lo = (r - mid.astype(F32)).astype(BF16)
    return hi, mid, lo


def _dot01(sel, x):
    hi, mid, lo = _split3(x)
    return (jnp.dot(sel, hi, preferred_element_type=F32) + jnp.dot(sel, mid, preferred_element_type=F32)
            + jnp.dot(sel, lo, preferred_element_type=F32))


def _normalise_rows(x_ref, nw_ref, xn_ref):
    nw = nw_ref[...]

    def body(c, carry):
        r0 = pl.multiple_of(c * NORM_ROWS, NORM_ROWS)
        x = x_ref[pl.ds(r0, NORM_ROWS), :]
        ms = jnp.mean(x * x, axis=-1, keepdims=True)
        xn_ref[pl.ds(r0, NORM_ROWS), :] = (x * lax.rsqrt(ms + EPS) * nw).astype(xn_ref.dtype)
        return carry

    lax.fori_loop(0, x_ref.shape[0] // NORM_ROWS, body, 0)


NORM_ROWS = 16


def _norm_mm_kernel(x_ref, nw_ref, w_ref, *rest, seg_lo, seg_hi, seg_width, seg_outs):
    out_refs, xn_ref = rest[:-1], rest[-1]
    j = pl.program_id(1)
    tn = w_ref.shape[1]

    @pl.when(j == 0)
    def _normalise():
        _normalise_rows(x_ref, nw_ref, xn_ref)

    acc = jnp.dot(xn_ref[...], w_ref[...], preferred_element_type=F32)

    k = 0
    for s, outs in enumerate(seg_outs):
        refs = out_refs[k:k + len(outs)]
        k += len(outs)

        def write(refs=refs, outs=outs, s=s):
            r = acc if seg_width[s] == tn else acc[:, :seg_width[s]]
            for o_ref, scale in zip(refs, outs):
                o_ref[...] = (r if scale == 1.0 else r * scale).astype(o_ref.dtype)

        if len(seg_outs) == 1:
            write()
        else:
            pl.when((j >= seg_lo[s]) & (j < seg_hi[s]))(write)


def _norm_mm(x, nw, w, segs, *, tn, tm=1024):
    m, kdim = x.shape
    n = w.shape[1]
    tm = min(m, tm)
    assert m % tm == 0 and n % tn == 0 and tm % NORM_ROWS == 0
    lo, seg_lo, seg_hi = 0, [], []
    for nt, _, _ in segs:
        seg_lo.append(lo)
        lo += nt
        seg_hi.append(lo)
    assert lo == n // tn
    out_shape, out_specs = [], []
    for (nt, width, outs), s_lo in zip(segs, seg_lo):
        for dtype, _ in outs:
            out_shape.append(jax.ShapeDtypeStruct((m, nt * width), dtype))
            out_specs.append(pl.BlockSpec(
                (tm, width), lambda i, j, s_lo=s_lo, nt=nt: (i, jnp.clip(j - s_lo, 0, nt - 1))))
    kern = functools.partial(
        _norm_mm_kernel, seg_lo=tuple(seg_lo), seg_hi=tuple(seg_hi),
        seg_width=tuple(s[1] for s in segs),
        seg_outs=tuple(tuple(float(sc) for _, sc in s[2]) for s in segs))
    return pl.pallas_call(
        kern,
        grid=(m // tm, n // tn),
        in_specs=[pl.BlockSpec((tm, kdim), lambda i, j: (i, 0)),
                  pl.BlockSpec((1, kdim), lambda i, j: (0, 0)),
                  pl.BlockSpec((kdim, tn), lambda i, j: (0, j))],
        out_specs=out_specs,
        out_shape=out_shape,
        scratch_shapes=[pltpu.VMEM((tm, kdim), BF16)],
        compiler_params=_cparams(2),
        name="norm_mm",
    )(x, nw.reshape(1, kdim), w)


def _mm_res_kernel(a_ref, w_ref, r_ref, o_ref):
    o_ref[...] = r_ref[...] + jnp.dot(a_ref[...], w_ref[...], preferred_element_type=F32)


def _mm_res(a, w, res, *, tn=512):
    m, kdim = a.shape
    n = w.shape[1]
    tm = min(m, 512)
    assert m % tm == 0 and n % tn == 0
    return pl.pallas_call(
        _mm_res_kernel,
        grid=(m // tm, n // tn),
        in_specs=[pl.BlockSpec((tm, kdim), lambda i, j: (i, 0)),
                  pl.BlockSpec((kdim, tn), lambda i, j: (0, j)),
                  pl.BlockSpec((tm, tn), lambda i, j: (i, j))],
        out_specs=pl.BlockSpec((tm, tn), lambda i, j: (i, j)),
        out_shape=jax.ShapeDtypeStruct((m, n), F32),
        compiler_params=_cparams(2),
        name="mm_res",
    )(a, w, res)


def _mlp_kernel(x_ref, nw_ref, wu_ref, wd_ref, o_ref, xn_ref):
    j = pl.program_id(1)

    @pl.when(j == 0)
    def _start():
        _normalise_rows(x_ref, nw_ref, xn_ref)
        o_ref[...] = x_ref[...]

    h = jnp.dot(xn_ref[...], wu_ref[...], preferred_element_type=F32)
    h = jnp.square(jnp.maximum(h, 0.0)).astype(BF16)
    o_ref[...] += jnp.dot(h, wd_ref[...], preferred_element_type=F32)


def _mlp(x, nw, w_up, w_down, *, tf=512):
    m, d = x.shape
    d_ff = w_up.shape[1]
    tm = min(m, 1024)
    assert m % tm == 0 and d_ff % tf == 0
    return pl.pallas_call(
        _mlp_kernel,
        grid=(m // tm, d_ff // tf),
        in_specs=[pl.BlockSpec((tm, d), lambda i, j: (i, 0)),
                  pl.BlockSpec((1, d), lambda i, j: (0, 0)),
                  pl.BlockSpec((d, tf), lambda i, j: (0, j)),
                  pl.BlockSpec((tf, d), lambda i, j: (j, 0))],
        out_specs=pl.BlockSpec((tm, d), lambda i, j: (i, 0)),
        out_shape=jax.ShapeDtypeStruct((m, d), F32),
        scratch_shapes=[pltpu.VMEM((tm, d), BF16)],
        compiler_params=_cparams(2),
        name="mlp",
    )(x, nw.reshape(1, d), w_up, w_down)


def _rmsnorm_kernel(x_ref, nw_ref, o_ref):
    _normalise_rows(x_ref, nw_ref, o_ref)


def _rmsnorm(x, nw):
    m, d = x.shape
    tm = min(m, 512)
    return pl.pallas_call(
        _rmsnorm_kernel,
        grid=(m // tm,),
        in_specs=[pl.BlockSpec((tm, d), lambda i: (i, 0)), pl.BlockSpec((1, d), lambda i: (0, 0))],
        out_specs=pl.BlockSpec((tm, d), lambda i: (i, 0)),
        out_shape=jax.ShapeDtypeStruct((m, d), F32),
        compiler_params=_cparams(1),
        name="final_rmsnorm",
    )(x, nw.reshape(1, d))


CONV_SLAB = 512
CONV_UNROLL = 4
HALO = V7X_BF16_ROWS


def _ssd_kernel(z_ref, xbc_ref, dt_ref, conv0_ref, h0_ref, cw_ref, cb_ref, dtb_ref, alog_ref,
                dsk_ref, nw_ref, e_ref, shift_ref, g_ref, convo_ref, ho_ref, xp_ref, ht_ref, xs_ref, bc_ref):
    t = pl.program_id(1)
    nt = pl.num_programs(1)
    L = z_ref.shape[0]
    d_inner = z_ref.shape[1]
    n_bc = bc_ref.shape[1]
    gw = d_inner // SSD_GROUPS
    hpg = gw // SSD_HEAD_DIM

    eye_r = lax.broadcasted_iota(jnp.int32, (V7X_LANES, V7X_LANES), 0)
    eye_c = lax.broadcasted_iota(jnp.int32, (V7X_LANES, V7X_LANES), 1)
    eye_b = (eye_r == eye_c).astype(F32).astype(BF16)

    @pl.when(t == 0)
    def _init():
        xp_ref[0:HALO, :] = conv0_ref[...]
        for g in range(SSD_GROUPS):
            ht_ref[g] = jnp.transpose(h0_ref[g])

    xp_ref[HALO:HALO + L, :] = xbc_ref[...]

    def conv_slab(c0):
        taps = jnp.dot(shift_ref[...], xp_ref[:, pl.ds(c0, CONV_SLAB)], preferred_element_type=F32)
        u = cb_ref[:, pl.ds(c0, CONV_SLAB)]
        for k in range(D_CONV):
            u = u + cw_ref[pl.ds(k, 1), pl.ds(c0, CONV_SLAB)] * taps[k * L:(k + 1) * L]
        return _silu(u)

    def x_body(c, carry):
        c0 = pl.multiple_of(c * CONV_SLAB, CONV_SLAB)
        xs_ref[:, pl.ds(c0, CONV_SLAB)] = conv_slab(c0)
        return carry

    lax.fori_loop(0, d_inner // CONV_SLAB, x_body, 0, unroll=CONV_UNROLL)

    def bc_body(c, carry):
        c0 = pl.multiple_of(c * CONV_SLAB, CONV_SLAB)
        bc_ref[:, pl.ds(c0, CONV_SLAB)] = conv_slab(d_inner + c0).astype(BF16)
        return carry

    lax.fori_loop(0, n_bc // CONV_SLAB, bc_body, 0, unroll=CONV_UNROLL)

    dt = _softplus(dt_ref[...] + dtb_ref[...])
    a = dt * (-jnp.exp(alog_ref[...]))
    ri = lax.broadcasted_iota(jnp.int32, (L, L), 0)
    ci = lax.broadcasted_iota(jnp.int32, (L, L), 1)
    tri = ri >= ci
    acum = _dot01(tri.astype(F32).astype(BF16), a)
    a_hi, a_mid, a_lo = _split3(acum)
    acum_t = (lax.dot_general(eye_b, a_hi, _NT, preferred_element_type=F32)
              + lax.dot_general(eye_b, a_mid, _NT, preferred_element_type=F32)
              + lax.dot_general(eye_b, a_lo, _NT, preferred_element_type=F32))
    last = acum[L - 1:L, :]
    per_head = jnp.concatenate(
        [dt.astype(BF16), jnp.exp(last - acum).astype(BF16), jnp.exp(acum).astype(BF16)], axis=0)
    cd8 =jnp.broadcast_to(jnp.exp(last), (V7X_SUBLANES, V7X_LANES))
    c_hi, c_mid, c_lo = _split3(cd8)
    lane = lax.broadcasted_iota(jnp.int32, (L, V7X_LANES), 1)
    lane_lo = lane < SSD_HEAD_DIM

    for g in range(SSD_GROUPS):
        cs = slice(g * gw, (g + 1) * gw)
        e_g = e_ref[:, cs]
        ex = jnp.dot(per_head, e_g, preferred_element_type=F32)
        dtx, tlx, eax = ex[0:L], ex[L:2 * L], ex[2 * L:3 * L]
        cdx = (jnp.dot(c_hi, e_g, preferred_element_type=F32) + jnp.dot(c_mid, e_g, preferred_element_type=F32)
               + jnp.dot(c_lo, e_g, preferred_element_type=F32))[0:1, :]
        xs_g = xs_ref[:, cs]
        xdt = xs_g * dtx
        b_g = bc_ref[:, g * D_STATE:(g + 1) * D_STATE]
        c_g = bc_ref[:, n_bc // 2 + g * D_STATE:n_bc // 2 + (g + 1) * D_STATE]
        cb = lax.dot_general(c_g, b_g, _NT, preferred_element_type=F32)
        ht = ht_ref[g]
        y_state = jnp.dot(c_g, ht.astype(BF16), preferred_element_type=F32) * eax
        b_gt = lax.dot_general(eye_b, b_g, _NT, preferred_element_type=F32).astype(BF16)
        ht_ref[g] = ht * cdx + jnp.dot(b_gt, (xdt * tlx).astype(BF16), preferred_element_type=F32)

        pairs = []
        for p in range(gw // V7X_LANES):
            ls = slice(p * V7X_LANES, (p + 1) * V7X_LANES)
            xdt_p = xdt[:, ls]
            y_p = y_state[:, ls] + xs_g[:, ls] * dsk_ref[:, g * gw + p * V7X_LANES:g * gw + (p + 1) * V7X_LANES]
            for q in range(V7X_LANES // SSD_HEAD_DIM):
                h = g * hpg + p * (V7X_LANES // SSD_HEAD_DIM) + q
                seg = acum[:, h:h + 1] - acum_t[h:h + 1, :]
                m = (jnp.exp(jnp.where(tri, seg, NEG)) * cb).astype(BF16)
                rhs = jnp.where(lane_lo if q == 0 else jnp.logical_not(lane_lo), xdt_p, 0.0).astype(BF16)
                y_p = y_p + jnp.dot(m, rhs, preferred_element_type=F32)
            pairs.append(y_p)
        y_g = jnp.concatenate(pairs, axis=1)
        gate = y_g * _silu(z_ref[:, cs].astype(F32))
        ms = jnp.mean(gate * gate, axis=-1, keepdims=True)
        g_ref[:, cs] = (gate * lax.rsqrt(ms + EPS) * nw_ref[:, cs]).astype(g_ref.dtype)

    xp_ref[0:HALO, :] = xp_ref[L:L + HALO, :]

    @pl.when(t == nt - 1)
    def _finish():
        convo_ref[...] = xp_ref[0:HALO, :].astype(F32)[HALO - (D_CONV - 1):HALO, :]
        for g in range(SSD_GROUPS):
            ho_ref[g] = jnp.transpose(ht_ref[g])


def _ssd_scan(z, xbc, dt, conv0, h0, conv_w, conv_b, dt_bias, a_log, d_skip, norm_w, *, chunk):
    bsz, seq, d_inner = z.shape
    conv_dim = xbc.shape[2]
    n_heads = d_inner // SSD_HEAD_DIM
    gw = d_inner // SSD_GROUPS
    L = min(chunk, seq)
    assert seq % L == 0 and n_heads <= V7X_LANES and L % HALO == 0
    pad = V7X_LANES - n_heads
    expand = np.zeros((V7X_LANES, d_inner), np.float32)
    expand[np.arange(d_inner) // SSD_HEAD_DIM, np.arange(d_inner)] = 1.0
    shift = np.zeros((D_CONV * L, L + HALO), np.float32)
    for k in range(D_CONV):
        shift[k * L + np.arange(L), HALO - (D_CONV - 1) + k + np.arange(L)] = 1.0
    conv0_p = jnp.pad(conv0.astype(BF16), ((0, 0), (HALO - (D_CONV - 1), 0), (0, 0)))
    row = lambda v: v.astype(F32).reshape(1, -1)
    full = lambda shape: pl.BlockSpec(shape, lambda b, t: (0,) * len(shape))
    g_out, conv_out, h_out = pl.pallas_call(
        _ssd_kernel,
        grid=(bsz, seq // L),
        in_specs=[pl.BlockSpec((None, L, d_inner), lambda b, t: (b, t, 0)),
                  pl.BlockSpec((None, L, conv_dim), lambda b, t: (b, t, 0)),
                  pl.BlockSpec((None, L, V7X_LANES), lambda b, t: (b, t, 0)),
                  pl.BlockSpec((None, HALO, conv_dim), lambda b, t: (b, 0, 0)),
                  pl.BlockSpec((None, SSD_GROUPS, gw, D_STATE), lambda b, t: (b, 0, 0, 0)),
                  full((D_CONV, conv_dim)), full((1, conv_dim)), full((1, V7X_LANES)), full((1, V7X_LANES)),
                  full((1, d_inner)), full((1, d_inner)), full((V7X_LANES, d_inner)),
                  full((D_CONV * L, L + HALO))],
        out_specs=[pl.BlockSpec((None, L, d_inner), lambda b, t: (b, t, 0)),
                   pl.BlockSpec((None, D_CONV - 1, conv_dim), lambda b, t: (b, 0, 0)),
                   pl.BlockSpec((None, SSD_GROUPS, gw, D_STATE), lambda b, t: (b, 0, 0, 0))],
        out_shape=[jax.ShapeDtypeStruct((bsz, seq, d_inner), BF16),
                   jax.ShapeDtypeStruct((bsz, D_CONV - 1, conv_dim), F32),
                   jax.ShapeDtypeStruct((bsz, SSD_GROUPS, gw, D_STATE), F32)],
        scratch_shapes=[pltpu.VMEM((L + HALO, conv_dim), BF16),
                        pltpu.VMEM((SSD_GROUPS, D_STATE, gw), F32),
                        pltpu.VMEM((L, d_inner), F32),
                        pltpu.VMEM((L, conv_dim - d_inner), BF16)],
        compiler_params=_cparams(2),
        name="ssd_scan",
    )(z, xbc, dt, conv0_p, h0.astype(F32).reshape(bsz, SSD_GROUPS, gw, D_STATE),
      conv_w.astype(F32), row(conv_b), jnp.pad(row(dt_bias), ((0, 0), (0, pad))),
      jnp.pad(row(a_log), ((0, 0), (0, pad))), row(jnp.repeat(d_skip, SSD_HEAD_DIM)), row(norm_w),
      jnp.asarray(expand, BF16), jnp.asarray(shift, BF16))
    return g_out, conv_out, h_out.reshape(bsz, n_heads, SSD_HEAD_DIM, D_STATE)


def _alibi_slopes():
    return 2.0 ** (-8.0 * np.arange(1, ATT_HEADS + 1, dtype=np.float64) / ATT_HEADS)


def _lam(lq_ref, lk_ref, lam_init):
    e = jnp.exp(jnp.sum(lq_ref[...] * lk_ref[...], axis=-1, keepdims=True))
    return e[0:1, :] - e[1:2, :] + lam_init


def _flash_kernel(slope_ref, q_ref, k_ref, v_ref, lq_ref, lk_ref, swt_ref, o_ref,
                  vt_ref, m_ref, l_ref, acc_ref, *, lam_init):
    h = pl.program_id(1)
    qi = pl.program_id(2)
    tq = q_ref.shape[0]
    tk = tq
    seq = k_ref.shape[0]
    dh = q_ref.shape[1] // 2
    slope = slope_ref[h]

    @pl.when(qi == 0)
    def _transpose_v():
        def body(c, carry):
            r0 = pl.multiple_of(c * tk, tk)
            vt_ref[:, pl.ds(r0, tk)] = jnp.transpose(v_ref[pl.ds(r0, tk), :].astype(F32)).astype(BF16)
            return carry
        lax.fori_loop(0, seq // tk, body, 0)

    m_ref[...] = jnp.full(m_ref.shape, NEG, F32)
    l_ref[...] = jnp.zeros(l_ref.shape, F32)
    acc_ref[...] = jnp.zeros(acc_ref.shape, F32)

    q = q_ref[...]
    kidx = lax.broadcasted_iota(jnp.int32, (tk, tq), 0)
    qidx = lax.broadcasted_iota(jnp.int32, (tk, tq), 1)
    rel = (qidx - kidx).astype(F32)

    def update(mi, s, vt_blk, off):
        m_prev = m_ref[mi]
        m_new = jnp.maximum(m_prev, jnp.max(s, axis=0, keepdims=True) - off)
        p = jnp.exp2(s - (m_new + off))
        alpha = jnp.exp2(m_prev - m_new)
        l_ref[mi] = alpha * l_ref[mi] + jnp.sum(p, axis=0, keepdims=True)
        acc_ref[mi] = alpha * acc_ref[mi] + jnp.dot(vt_blk, p.astype(BF16), preferred_element_type=F32)
        m_ref[mi] = m_new

    bias = slope * rel

    def body(kj, carry):
        r0 = pl.multiple_of(kj * tk, tk)
        kblk = k_ref[pl.ds(r0, tk), :]
        vt_blk = vt_ref[:, pl.ds(r0, tk)]
        off = slope * ((qi - kj) * tq).astype(F32)
        for mi in range(2):
            s = lax.dot_general(kblk[:, mi * dh:(mi + 1) * dh], q[:, mi * dh:(mi + 1) * dh], _NT,
                                preferred_element_type=F32) - bias
            update(mi, s, vt_blk, off)
        return carry

    lax.fori_loop(0, qi, body, 0)

    r0 = pl.multiple_of(qi * tk, tk)
    kblk = k_ref[pl.ds(r0, tk), :]
    vt_blk = vt_ref[:, pl.ds(r0, tk)]
    visible = (kidx >> CHUNK_SHIFT) <= (qidx >> CHUNK_SHIFT)
    bias_d = slope * jnp.abs(rel)
    for mi in range(2):
        s = lax.dot_general(kblk[:, mi * dh:(mi + 1) * dh], q[:, mi * dh:(mi + 1) * dh], _NT,
                            preferred_element_type=F32) - bias_d
        update(mi, jnp.where(visible, s, NEG), vt_blk, 0.0)

    o = acc_ref[0] / l_ref[0] - _lam(lq_ref, lk_ref, lam_init) * (acc_ref[1] / l_ref[1])
    ms = jnp.mean(o * o, axis=0, keepdims=True)
    o = o * lax.rsqrt(ms + EPS) * swt_ref[...] * (1.0 - lam_init)
    o_ref[...] = jnp.transpose(o).astype(o_ref.dtype)


def _flash_attention(q, k, v, lam_q, lam_k, subln_w, lam_init, *, tq=512):
    bsz, seq, width = q.shape
    hw = width // ATT_HEADS
    tq = min(tq, seq)
    assert seq % tq == 0 and tq % CHUNK == 0
    smem = pl.BlockSpec(memory_space=pltpu.SMEM)
    full = lambda shape: pl.BlockSpec(shape, lambda b, h, i: (0,) * len(shape))
    swt = jnp.broadcast_to(subln_w.astype(F32)[:, None], (hw, tq))
    return pl.pallas_call(
        functools.partial(_flash_kernel, lam_init=lam_init),
        grid=(bsz, ATT_HEADS, seq // tq),
        in_specs=[smem,
                  pl.BlockSpec((None, tq, hw), lambda b, h, i: (b, i, h)),
                  pl.BlockSpec((None, seq, hw), lambda b, h, i: (b, 0, h)),
                  pl.BlockSpec((None, seq, hw), lambda b, h, i: (b, 0, h)),
                  full(lam_q.shape), full(lam_k.shape), full((hw, tq))],
        out_specs=pl.BlockSpec((None, tq, hw), lambda b, h, i: (b, i, h)),
        out_shape=jax.ShapeDtypeStruct((bsz, seq, width), BF16),
        scratch_shapes=[pltpu.VMEM((hw, seq), BF16),
                        pltpu.VMEM((2, 1, tq), F32), pltpu.VMEM((2, 1, tq), F32),
                        pltpu.VMEM((2, hw, tq), F32)],
        compiler_params=_cparams(3),
        name="flash_diff_attn",
    )(jnp.asarray(_alibi_slopes() * LOG2E, F32), q, k, v, lam_q.astype(F32), lam_k.astype(F32), swt)


DECODE_PAST_BLOCK = 512


def _decode_kernel(q_ref, kn_ref, vn_ref, kc_ref, vc_ref, srow_ref, lq_ref, lk_ref, sw_ref, o_ref,
                   mb_ref, m_ref, l_ref, acc_ref, *, lam_init, past):
    pi = pl.program_id(1)
    npb = pl.num_programs(1)
    tq = q_ref.shape[0]
    rows = ATT_HEADS * tq
    cols = kc_ref.shape[0]
    pblk = cols // ATT_HEADS
    hw = sw_ref.shape[1]
    dh = hw // 2
    t_shift = tq.bit_length() - 1
    h_shift = ATT_HEADS.bit_length() - 1
    srow = srow_ref[:, 0:1]

    @pl.when(pi == 0)
    def _init():
        m_ref[...] = jnp.full(m_ref.shape, NEG, F32)
        l_ref[...] = jnp.zeros(l_ref.shape, F32)
        acc_ref[...] = jnp.zeros(acc_ref.shape, F32)
        r = lax.broadcasted_iota(jnp.int32, (rows, cols), 0)
        c = lax.broadcasted_iota(jnp.int32, (rows, cols), 1)
        same_head = (c & (ATT_HEADS - 1)) == (r >> t_shift)
        rel = ((r & (tq - 1)) - (c >> h_shift)).astype(F32)
        mb_ref[...] = jnp.where(same_head, -(srow * rel), NEG)

    def update(mi, rs, s, vblk):
        m_prev = m_ref[mi, rs]
        m_new = jnp.maximum(m_prev, jnp.max(s, axis=-1, keepdims=True))
        p = jnp.exp2(s - m_new)
        alpha = jnp.exp2(m_prev - m_new)
        l_ref[mi, rs] = alpha * l_ref[mi, rs] + jnp.sum(p, axis=-1, keepdims=True)
        acc_ref[mi, rs] = alpha * acc_ref[mi, rs] + jnp.dot(p.astype(BF16), vblk, preferred_element_type=F32)
        m_ref[mi, rs] = m_new

    off = srow * (past - pi * pblk).astype(F32)
    kc = kc_ref[...].astype(BF16)
    vc = vc_ref[...].astype(BF16)
    for mi in range(2):
        qm = jnp.concatenate([q_ref[:, h * hw + mi * dh:h * hw + (mi + 1) * dh] for h in range(ATT_HEADS)], axis=0)
        s = lax.dot_general(qm, kc[:, mi * dh:(mi + 1) * dh], _NT, preferred_element_type=F32)
        update(mi, slice(None), s + mb_ref[...] - off, vc)

    @pl.when(pi == npb - 1)
    def _finish():
        slopes = _alibi_slopes() * LOG2E
        ri = lax.broadcasted_iota(jnp.int32, (tq, tq), 0)
        ci = lax.broadcasted_iota(jnp.int32, (tq, tq), 1)
        dist_n = jnp.abs(ri - ci).astype(F32)
        vis_n = ((past + ci) >> CHUNK_SHIFT) <= ((past + ri) >> CHUNK_SHIFT)
        lam = _lam(lq_ref, lk_ref, lam_init)
        for h in range(ATT_HEADS):
            hs = slice(h * hw, (h + 1) * hw)
            rs = slice(h * tq, (h + 1) * tq)
            kn = kn_ref[:, hs].astype(BF16)
            vn = vn_ref[:, hs].astype(BF16)
            bias = float(slopes[h]) * dist_n
            for mi in range(2):
                qm = q_ref[:, h * hw + mi * dh:h * hw + (mi + 1) * dh]
                s = lax.dot_general(qm, kn[:, mi * dh:(mi + 1) * dh], _NT, preferred_element_type=F32) - bias
                update(mi, rs, jnp.where(vis_n, s, NEG), vn)
            o = acc_ref[0, rs] / l_ref[0, rs] - lam * (acc_ref[1, rs] / l_ref[1, rs])
            ms = jnp.mean(o * o, axis=-1, keepdims=True)
            o_ref[:, hs] = (o * lax.rsqrt(ms + EPS) * sw_ref[...] * (1.0 - lam_init)).astype(o_ref.dtype)


def _decode_attention(q, k_new, v_new, cache_k, cache_v, layer, lam_q, lam_k, subln_w, lam_init):
    bsz, tq, width = q.shape
    hw = width // ATT_HEADS
    past = cache_k.shape[2] // ATT_HEADS
    pblk = min(past, DECODE_PAST_BLOCK)
    rows = ATT_HEADS * tq
    assert past % pblk == 0 and tq & (tq - 1) == 0 and ATT_HEADS & (ATT_HEADS - 1) == 0
    srow = np.repeat(_alibi_slopes() * LOG2E, tq)[:, None] * np.ones((1, V7X_LANES))
    full = lambda shape: pl.BlockSpec(shape, lambda b, p: (0,) * len(shape))
    new = pl.BlockSpec((None, tq, width), lambda b, p: (b, 0, 0))
    cache = pl.BlockSpec((None, None, pblk * ATT_HEADS, hw), lambda b, p: (layer, b, p, 0))
    return pl.pallas_call(
        functools.partial(_decode_kernel, lam_init=lam_init, past=past),
        grid=(bsz, past // pblk),
        in_specs=[new, new, new, cache, cache, full((rows, V7X_LANES)), full(lam_q.shape), full(lam_k.shape),
                  full((1, hw))],
        out_specs=new,
        out_shape=jax.ShapeDtypeStruct((bsz, tq, width), BF16),
        scratch_shapes=[pltpu.VMEM((rows, pblk * ATT_HEADS), F32),
                        pltpu.VMEM((2, rows, 1), F32), pltpu.VMEM((2, rows, 1), F32),
                        pltpu.VMEM((2, rows, hw), F32)],
        compiler_params=_cparams(2),
        name="decode_diff_attn",
    )(q, k_new, v_new, cache_k, cache_v, jnp.asarray(srow, F32), lam_q.astype(F32), lam_k.astype(F32),
      subln_w.astype(F32).reshape(1, hw))


MM_TN = 512
SSD_CHUNK = 128


def _ssd_layer(h, bsz, norm_w, w_in_p, conv0, h0, conv_w, conv_b, dt_bias, a_log, d_skip, gnorm_w, w_out,
               d_inner, conv_dim):
    m = h.shape[0]
    seq = m // bsz
    z, xbc, dt = _norm_mm(
        h, norm_w, w_in_p,
        [(d_inner // MM_TN, MM_TN, [(BF16, 1.0)]), (conv_dim // MM_TN, MM_TN, [(BF16, 1.0)]),
         (1, V7X_LANES, [(F32, 1.0)])],
        tn=MM_TN)
    g, conv_new, h_new = _ssd_scan(
        z.reshape(bsz, seq, d_inner), xbc.reshape(bsz, seq, conv_dim), dt.reshape(bsz, seq, V7X_LANES),
        conv0, h0, conv_w, conv_b, dt_bias, a_log, d_skip, gnorm_w, chunk=SSD_CHUNK)
    return _mm_res(g.reshape(m, d_inner), w_out, h), conv_new, h_new


def _qkv(h, norm_w, w_qkv, width):
    dh = width // (2 * ATT_HEADS)
    nt = width // MM_TN
    kv = [(BF16, 1.0), (F32, 1.0)]
    return _norm_mm(h, norm_w, w_qkv, [(nt, MM_TN, [(BF16, dh ** -0.5 * LOG2E)]), (nt, MM_TN, kv), (nt, MM_TN, kv)],
                    tn=MM_TN, tm=512)


def kernel(x_prompt, x_sample, cache_k, cache_v, state_ssm, state_conv, norm_mix_w, norm_mlp_w, final_norm_w, ssd_w_in, ssd_conv_w, ssd_conv_b, ssd_dt_bias, ssd_a_log, ssd_d, ssd_norm_w, ssd_w_out, att_w_qkv, att_lam_q, att_lam_k, att_subln_w, att_w_o, mlp_w_up, mlp_w_down):
    bp, sp, d_model = x_prompt.shape
    bs, ss, _ = x_sample.shape
    depth = norm_mix_w.shape[0]
    d_inner = ssd_w_out.shape[1]
    conv_dim = ssd_conv_w.shape[2]
    n_heads = ssd_d.shape[1]
    att_width = att_w_o.shape[1]
    hw = att_width // ATT_HEADS

    hp = x_prompt.reshape(bp * sp, d_model)
    hs = x_sample.reshape(bs * ss, d_model)
    cache_k = cache_k.reshape(cache_k.shape[0], bs, -1, hw)
    cache_v = cache_v.reshape(cache_v.shape[0], bs, -1, hw)

    k_p, v_p, ssm_p, conv_p, k_s, v_s, ssm_s, conv_s = [], [], [], [], [], [], [], []
    for i in range(depth):
        j = i // 2
        if i % 2 == 0:
            w_in = ssd_w_in[j]
            w_in_p = jnp.concatenate(
                [w_in[:, :d_inner + conv_dim],
                 jnp.pad(w_in[:, d_inner + conv_dim:], ((0, 0), (0, MM_TN - n_heads)))], axis=1).astype(BF16)
            w_out = ssd_w_out[j].astype(BF16)
            args = (ssd_conv_w[j], ssd_conv_b[j], ssd_dt_bias[j], ssd_a_log[j], ssd_d[j], ssd_norm_w[j], w_out,
                    d_inner, conv_dim)
            hp, cp, sp_state = _ssd_layer(
                hp, bp, norm_mix_w[i], w_in_p, jnp.zeros((bp, D_CONV - 1, conv_dim), F32),
                jnp.zeros((bp, n_heads, SSD_HEAD_DIM, D_STATE), F32), *args)
            hs, cs, ss_state = _ssd_layer(hs, bs, norm_mix_w[i], w_in_p, state_conv[j], state_ssm[j], *args)
            conv_p.append(cp)
            ssm_p.append(sp_state)
            conv_s.append(cs)
            ssm_s.append(ss_state)
        else:
            w_qkv = att_w_qkv[j].astype(BF16)
            w_o = att_w_o[j].astype(BF16)
            lam_init = 0.8 - 0.6 * math.exp(-0.3 * i)
            qp, kpb, kp, vpb, vp = _qkv(hp, norm_mix_w[i], w_qkv, att_width)
            op = _flash_attention(qp.reshape(bp, sp, att_width), kpb.reshape(bp, sp, att_width),
                                  vpb.reshape(bp, sp, att_width), att_lam_q[j], att_lam_k[j], att_subln_w[j], lam_init)
            hp = _mm_res(op.reshape(bp * sp, att_width), w_o, hp)
            qs, _, ks, _, vs = _qkv(hs, norm_mix_w[i], w_qkv, att_width)
            os_ = _decode_attention(qs.reshape(bs, ss, att_width), ks.reshape(bs, ss, att_width),
                                    vs.reshape(bs, ss, att_width), cache_k, cache_v, j,
                                    att_lam_q[j], att_lam_k[j], att_subln_w[j], lam_init)
            hs = _mm_res(os_.reshape(bs * ss, att_width), w_o, hs)
            k_p.append(kp.reshape(bp, sp, ATT_HEADS, hw))
            v_p.append(vp.reshape(bp, sp, ATT_HEADS, hw))
            k_s.append(ks.reshape(bs, ss, ATT_HEADS, hw))
            v_s.append(vs.reshape(bs, ss, ATT_HEADS, hw))
        w_up = mlp_w_up[i].astype(BF16)
        w_down = mlp_w_down[i].astype(BF16)
        hp = _mlp(hp, norm_mlp_w[i], w_up, w_down)
        hs = _mlp(hs, norm_mlp_w[i], w_up, w_down)

    y_prompt = _rmsnorm(hp, final_norm_w).reshape(bp, sp, d_model)
    y_sample = _rmsnorm(hs, final_norm_w).reshape(bs, ss, d_model)
    return (y_prompt, y_sample, jnp.stack(k_p), jnp.stack(v_p), jnp.stack(ssm_p), jnp.stack(conv_p),
            jnp.stack(k_s), jnp.stack(v_s), jnp.stack(ssm_s), jnp.stack(conv_s))
```

```python
import functools
import math

import numpy as np
import jax
import jax.numpy as jnp
from jax import lax
from jax.experimental import pallas as pl
from jax.experimental.pallas import tpu as pltpu

F32 = jnp.float32
BF16 = jnp.bfloat16

EPS = 1e-5
CHUNK = 64
CHUNK_SHIFT = 6
D_CONV = 4
SSD_HEAD_DIM = 64
SSD_GROUPS = 8
D_STATE = 128
ATT_HEADS = 8
NEG = -1e30
LOG2E = math.log2(math.e)

V7X_LANES = 128
V7X_SUBLANES = 8
V7X_BF16_ROWS = 16
V7X_VMEM_LIMIT_BYTES = 56 * 1024 * 1024

_NT = (((1,), (1,)), ((), ()))


def _cparams(n_axes):
    return pltpu.CompilerParams(dimension_semantics=("arbitrary",) * n_axes,
                                vmem_limit_bytes=V7X_VMEM_LIMIT_BYTES)


def _silu(u):
    return u * (1.0 / (1.0 + jnp.exp(-u)))


def _softplus(u):
    return jnp.maximum(u, 0.0) + jnp.log1p(jnp.exp(-jnp.abs(u)))


def _split3(x):
    hi = x.astype(BF16)
    r = x - hi.astype(F32)
    mid = r.astype(BF16)
    lo = (r - mid.astype(F32)).astype(BF16)
    return hi, mid, lo


def _dot01(sel, x):
    hi, mid, lo = _split3(x)
    return (jnp.dot(sel, hi, preferred_element_type=F32) + jnp.dot(sel, mid, preferred_element_type=F32)
            + jnp.dot(sel, lo, preferred_element_type=F32))


def _normalise_rows(x_ref, nw_ref, xn_ref):
    nw = nw_ref[...]

    def body(c, carry):
        r0 = pl.multiple_of(c * NORM_ROWS, NORM_ROWS)
        x = x_ref[pl.ds(r0, NORM_ROWS), :]
        ms = jnp.mean(x * x, axis=-1, keepdims=True)
        xn_ref[pl.ds(r0, NORM_ROWS), :] = (x * lax.rsqrt(ms + EPS) * nw).astype(xn_ref.dtype)
        return carry

    lax.fori_loop(0, x_ref.shape[0] // NORM_ROWS, body, 0)


NORM_ROWS = 16


def _norm_mm_kernel(*refs, seg_lo, seg_hi, seg_outs, has_side):
    n_in = 4 if has_side else 3
    x_ref, nw_ref, w_ref = refs[:3]
    out_refs, xn_ref = refs[n_in:-1], refs[-1]
    j = pl.program_id(1)

    @pl.when(j == 0)
    def _normalise():
        _normalise_rows(x_ref, nw_ref, xn_ref)
        if has_side:
            out_refs[-1][...] = jnp.dot(xn_ref[...], refs[3][...], preferred_element_type=F32)

    acc = jnp.dot(xn_ref[...], w_ref[...], preferred_element_type=F32)

    k = 0
    for s, outs in enumerate(seg_outs):
        seg_refs = out_refs[k:k + len(outs)]
        k += len(outs)

        def write(seg_refs=seg_refs, outs=outs):
            for o_ref, scale in zip(seg_refs, outs):
                o_ref[...] = (acc if scale == 1.0 else acc * scale).astype(o_ref.dtype)

        if len(seg_outs) == 1:
            write()
        else:
            pl.when((j >= seg_lo[s]) & (j < seg_hi[s]))(write)


def _norm_mm(x, nw, w, segs, *, tn, w_side=None):
    m, kdim = x.shape
    n = w.shape[1]
    tm = min(m, 1024)
    assert m % tm == 0 and n % tn == 0 and tm % NORM_ROWS == 0
    lo, seg_lo, seg_hi = 0, [], []
    for nt, _ in segs:
        seg_lo.append(lo)
        lo += nt
        seg_hi.append(lo)
    assert lo == n // tn
    out_shape, out_specs = [], []
    for (nt, outs), s_lo in zip(segs, seg_lo):
        for dtype, _ in outs:
            out_shape.append(jax.ShapeDtypeStruct((m, nt * tn), dtype))
            out_specs.append(pl.BlockSpec(
                (tm, tn), lambda i, j, s_lo=s_lo, nt=nt: (i, jnp.clip(j - s_lo, 0, nt - 1))))
    in_specs = [pl.BlockSpec((tm, kdim), lambda i, j: (i, 0)),
                pl.BlockSpec((1, kdim), lambda i, j: (0, 0)),
                pl.BlockSpec((kdim, tn), lambda i, j: (0, j))]
    args = [x, nw.reshape(1, kdim), w]
    if w_side is not None:
        in_specs.append(pl.BlockSpec(w_side.shape, lambda i, j: (0, 0)))
        args.append(w_side)
        out_shape.append(jax.ShapeDtypeStruct((m, w_side.shape[1]), F32))
        out_specs.append(pl.BlockSpec((tm, w_side.shape[1]), lambda i, j: (i, 0)))
    kern = functools.partial(
        _norm_mm_kernel, seg_lo=tuple(seg_lo), seg_hi=tuple(seg_hi),
        seg_outs=tuple(tuple(float(sc) for _, sc in s[1]) for s in segs), has_side=w_side is not None)
    return pl.pallas_call(
        kern,
        grid=(m // tm, n // tn),
        in_specs=in_specs,
        out_specs=out_specs,
        out_shape=out_shape,
        scratch_shapes=[pltpu.VMEM((tm, kdim), BF16)],
        compiler_params=_cparams(2),
        name="norm_mm",
    )(*args)


def _mm_res_kernel(a_ref, w_ref, r_ref, o_ref):
    o_ref[...] = r_ref[...] + jnp.dot(a_ref[...], w_ref[...], preferred_element_type=F32)


MM_RES_LHS_BLOCK_BYTES = 2 * 1024 * 1024


def _mm_res(a, w, res):
    m, kdim = a.shape
    n = w.shape[1]
    tm = min(m, MM_RES_LHS_BLOCK_BYTES // (kdim * a.dtype.itemsize))
    assert m % tm == 0
    return pl.pallas_call(
        _mm_res_kernel,
        grid=(m // tm,),
        in_specs=[pl.BlockSpec((tm, kdim), lambda i: (i, 0)),
                  pl.BlockSpec((kdim, n), lambda i: (0, 0), pipeline_mode=pl.Buffered(1)),
                  pl.BlockSpec((tm, n), lambda i: (i, 0))],
        out_specs=pl.BlockSpec((tm, n), lambda i: (i, 0)),
        out_shape=jax.ShapeDtypeStruct((m, n), F32),
        compiler_params=_cparams(1),
        name="mm_res",
    )(a, w, res)


def _mlp_kernel(x_ref, nw_ref, wu_ref, wd_ref, *rest, final_norm):
    fw_ref = rest[0] if final_norm else None
    o_ref, xn_ref = rest[-2], rest[-1]
    j = pl.program_id(1)

    @pl.when(j == 0)
    def _start():
        _normalise_rows(x_ref, nw_ref, xn_ref)
        o_ref[...] = x_ref[...]

    h = jnp.dot(xn_ref[...], wu_ref[...], preferred_element_type=F32)
    h = jnp.square(jnp.maximum(h, 0.0)).astype(BF16)
    o_ref[...] += jnp.dot(h, wd_ref[...], preferred_element_type=F32)

    if final_norm:
        @pl.when(j == pl.num_programs(1) - 1)
        def _final_norm():
            _normalise_rows(o_ref, fw_ref, o_ref)


def _mlp(x, nw, w_up, w_down, final_w=None, *, tf=512):
    m, d = x.shape
    d_ff = w_up.shape[1]
    tm = min(m, 1024)
    assert m % tm == 0 and d_ff % tf == 0
    in_specs = [pl.BlockSpec((tm, d), lambda i, j: (i, 0)),
                pl.BlockSpec((1, d), lambda i, j: (0, 0)),
                pl.BlockSpec((d, tf), lambda i, j: (0, j)),
                pl.BlockSpec((tf, d), lambda i, j: (j, 0))]
    args = [x, nw.reshape(1, d), w_up, w_down]
    if final_w is not None:
        in_specs.append(pl.BlockSpec((1, d), lambda i, j: (0, 0)))
        args.append(final_w.reshape(1, d))
    return pl.pallas_call(
        functools.partial(_mlp_kernel, final_norm=final_w is not None),
        grid=(m // tm, d_ff // tf),
        in_specs=in_specs,
        out_specs=pl.BlockSpec((tm, d), lambda i, j: (i, 0)),
        out_shape=jax.ShapeDtypeStruct((m, d), F32),
        scratch_shapes=[pltpu.VMEM((tm, d), BF16)],
        compiler_params=_cparams(2),
        name="mlp",
    )(*args)


CONV_SLAB = 512
CONV_UNROLL = 4
HALO = V7X_BF16_ROWS


def _ssd_kernel(z_ref, xbc_ref, dt_ref, conv0_ref, h0_ref, cw_ref, cb_ref, dtb_ref, alog_ref,
                dsk_ref, nw_ref, e_ref, shift_ref, g_ref, convo_ref, ho_ref, xp_ref, ht_ref, xs_ref, bc_ref):
    t = pl.program_id(1)
    nt = pl.num_programs(1)
    L = z_ref.shape[0]
    d_inner = z_ref.shape[1]
    n_bc = bc_ref.shape[1]
    gw = d_inner // SSD_GROUPS
    hpg = gw // SSD_HEAD_DIM

    eye_r = lax.broadcasted_iota(jnp.int32, (V7X_LANES, V7X_LANES), 0)
    eye_c = lax.broadcasted_iota(jnp.int32, (V7X_LANES, V7X_LANES), 1)
    eye_b = (eye_r == eye_c).astype(F32).astype(BF16)

    @pl.when(t == 0)
    def _init():
        xp_ref[0:HALO, :] = conv0_ref[...]
        for g in range(SSD_GROUPS):
            ht_ref[g] = jnp.transpose(h0_ref[g])

    xp_ref[HALO:HALO + L, :] = xbc_ref[...]

    def conv_slab(c0):
        taps = jnp.dot(shift_ref[...], xp_ref[:, pl.ds(c0, CONV_SLAB)], preferred_element_type=F32)
        u = cb_ref[:, pl.ds(c0, CONV_SLAB)]
        for k in range(D_CONV):
            u = u + cw_ref[pl.ds(k, 1), pl.ds(c0, CONV_SLAB)] * taps[k * L:(k + 1) * L]
        return _silu(u)

    def x_body(c, carry):
        c0 = pl.multiple_of(c * CONV_SLAB, CONV_SLAB)
        xs_ref[:, pl.ds(c0, CONV_SLAB)] = conv_slab(c0)
        return carry

    lax.fori_loop(0, d_inner // CONV_SLAB, x_body, 0, unroll=CONV_UNROLL)

    def bc_body(c, carry):
        c0 = pl.multiple_of(c * CONV_SLAB, CONV_SLAB)
        bc_ref[:, pl.ds(c0, CONV_SLAB)] = conv_slab(d_inner + c0).astype(BF16)
        return carry

    lax.fori_loop(0, n_bc // CONV_SLAB, bc_body, 0, unroll=CONV_UNROLL)

    dt = _softplus(dt_ref[...] + dtb_ref[...])
    a = dt * (-jnp.exp(alog_ref[...]))
    ri = lax.broadcasted_iota(jnp.int32, (L, L), 0)
    ci = lax.broadcasted_iota(jnp.int32, (L, L), 1)
    tri = ri >= ci
    acum = _dot01(tri.astype(F32).astype(BF16), a)
    a_hi, a_mid, a_lo = _split3(acum)
    acum_t = (lax.dot_general(eye_b, a_hi, _NT, preferred_element_type=F32)
              + lax.dot_general(eye_b, a_mid, _NT, preferred_element_type=F32)
              + lax.dot_general(eye_b, a_lo, _NT, preferred_element_type=F32))
    last = acum[L - 1:L, :]
    per_head = jnp.concatenate(
        [dt.astype(BF16), jnp.exp(last - acum).astype(BF16), jnp.exp(acum).astype(BF16)], axis=0)
    cd8 =jnp.broadcast_to(jnp.exp(last), (V7X_SUBLANES, V7X_LANES))
    c_hi, c_mid, c_lo = _split3(cd8)
    lane = lax.broadcasted_iota(jnp.int32, (L, V7X_LANES), 1)
    lane_lo = lane < SSD_HEAD_DIM

    for g in range(SSD_GROUPS):
        cs = slice(g * gw, (g + 1) * gw)
        e_g = e_ref[:, cs]
        ex = jnp.dot(per_head, e_g, preferred_element_type=F32)
        dtx, tlx, eax = ex[0:L], ex[L:2 * L], ex[2 * L:3 * L]
        cdx = (jnp.dot(c_hi, e_g, preferred_element_type=F32) + jnp.dot(c_mid, e_g, preferred_element_type=F32)
               + jnp.dot(c_lo, e_g, preferred_element_type=F32))[0:1, :]
        xs_g = xs_ref[:, cs]
        xdt = xs_g * dtx
        b_g = bc_ref[:, g * D_STATE:(g + 1) * D_STATE]
        c_g = bc_ref[:, n_bc // 2 + g * D_STATE:n_bc // 2 + (g + 1) * D_STATE]
        cb = lax.dot_general(c_g, b_g, _NT, preferred_element_type=F32)
        ht = ht_ref[g]
        y_state = jnp.dot(c_g, ht.astype(BF16), preferred_element_type=F32) * eax
        b_gt = lax.dot_general(eye_b, b_g, _NT, preferred_element_type=F32).astype(BF16)
        ht_ref[g] = ht * cdx + jnp.dot(b_gt, (xdt * tlx).astype(BF16), preferred_element_type=F32)

        pairs = []
        for p in range(gw // V7X_LANES):
            ls = slice(p * V7X_LANES, (p + 1) * V7X_LANES)
            xdt_p = xdt[:, ls]
            y_p = y_state[:, ls] + xs_g[:, ls] * dsk_ref[:, g * gw + p * V7X_LANES:g * gw + (p + 1) * V7X_LANES]
            for q in range(V7X_LANES // SSD_HEAD_DIM):
                h = g * hpg + p * (V7X_LANES // SSD_HEAD_DIM) + q
                seg = acum[:, h:h + 1] - acum_t[h:h + 1, :]
                m = (jnp.exp(jnp.where(tri, seg, NEG)) * cb).astype(BF16)
                rhs = jnp.where(lane_lo if q == 0 else jnp.logical_not(lane_lo), xdt_p, 0.0).astype(BF16)
                y_p = y_p + jnp.dot(m, rhs, preferred_element_type=F32)
            pairs.append(y_p)
        y_g = jnp.concatenate(pairs, axis=1)
        gate = y_g * _silu(z_ref[:, cs].astype(F32))
        ms = jnp.mean(gate * gate, axis=-1, keepdims=True)
        g_ref[:, cs] = (gate * lax.rsqrt(ms + EPS) * nw_ref[:, cs]).astype(g_ref.dtype)

    xp_ref[0:HALO, :] = xp_ref[L:L + HALO, :]

    @pl.when(t == nt - 1)
    def _finish():
        convo_ref[...] = xp_ref[0:HALO, :].astype(F32)[HALO - (D_CONV - 1):HALO, :]
        for g in range(SSD_GROUPS):
            ho_ref[g] = jnp.transpose(ht_ref[g])


def _ssd_scan(z, xbc, dt, conv0, h0, conv_w, conv_b, dt_bias, a_log, d_skip, norm_w, *, chunk):
    bsz, seq, d_inner = z.shape
    conv_dim = xbc.shape[2]
    n_heads = d_inner // SSD_HEAD_DIM
    gw = d_inner // SSD_GROUPS
    L = min(chunk, seq)
    assert seq % L == 0 and n_heads <= V7X_LANES and L % HALO == 0
    pad = V7X_LANES - n_heads
    expand = np.zeros((V7X_LANES, d_inner), np.float32)
    expand[np.arange(d_inner) // SSD_HEAD_DIM, np.arange(d_inner)] = 1.0
    shift = np.zeros((D_CONV * L, L + HALO), np.float32)
    for k in range(D_CONV):
        shift[k * L + np.arange(L), HALO - (D_CONV - 1) + k + np.arange(L)] = 1.0
    conv0_p = jnp.pad(conv0.astype(BF16), ((0, 0), (HALO - (D_CONV - 1), 0), (0, 0)))
    row = lambda v: v.astype(F32).reshape(1, -1)
    full = lambda shape: pl.BlockSpec(shape, lambda b, t: (0,) * len(shape))
    g_out, conv_out, h_out = pl.pallas_call(
        _ssd_kernel,
        grid=(bsz, seq // L),
        in_specs=[pl.BlockSpec((None, L, d_inner), lambda b, t: (b, t, 0)),
                  pl.BlockSpec((None, L, conv_dim), lambda b, t: (b, t, 0)),
                  pl.BlockSpec((None, L, V7X_LANES), lambda b, t: (b, t, 0)),
                  pl.BlockSpec((None, HALO, conv_dim), lambda b, t: (b, 0, 0)),
                  pl.BlockSpec((None, SSD_GROUPS, gw, D_STATE), lambda b, t: (b, 0, 0, 0)),
                  full((D_CONV, conv_dim)), full((1, conv_dim)), full((1, V7X_LANES)), full((1, V7X_LANES)),
                  full((1, d_inner)), full((1, d_inner)), full((V7X_LANES, d_inner)),
                  full((D_CONV * L, L + HALO))],
        out_specs=[pl.BlockSpec((None, L, d_inner), lambda b, t: (b, t, 0)),
                   pl.BlockSpec((None, D_CONV - 1, conv_dim), lambda b, t: (b, 0, 0)),
                   pl.BlockSpec((None, SSD_GROUPS, gw, D_STATE), lambda b, t: (b, 0, 0, 0))],
        out_shape=[jax.ShapeDtypeStruct((bsz, seq, d_inner), BF16),
                   jax.ShapeDtypeStruct((bsz, D_CONV - 1, conv_dim), F32),
                   jax.ShapeDtypeStruct((bsz, SSD_GROUPS, gw, D_STATE), F32)],
        scratch_shapes=[pltpu.VMEM((L + HALO, conv_dim), BF16),
                        pltpu.VMEM((SSD_GROUPS, D_STATE, gw), F32),
                        pltpu.VMEM((L, d_inner), F32),
                        pltpu.VMEM((L, conv_dim - d_inner), BF16)],
        compiler_params=_cparams(2),
        name="ssd_scan",
    )(z, xbc, dt, conv0_p, h0.astype(F32).reshape(bsz, SSD_GROUPS, gw, D_STATE),
      conv_w.astype(F32), row(conv_b), jnp.pad(row(dt_bias), ((0, 0), (0, pad))),
      jnp.pad(row(a_log), ((0, 0), (0, pad))), row(jnp.repeat(d_skip, SSD_HEAD_DIM)), row(norm_w),
      jnp.asarray(expand, BF16), jnp.asarray(shift, BF16))
    return g_out, conv_out, h_out.reshape(bsz, n_heads, SSD_HEAD_DIM, D_STATE)


def _alibi_slopes():
    return 2.0 ** (-8.0 * np.arange(1, ATT_HEADS + 1, dtype=np.float64) / ATT_HEADS)


def _lam(lq_ref, lk_ref, lam_init):
    e = jnp.exp(jnp.sum(lq_ref[...] * lk_ref[...], axis=-1, keepdims=True))
    return e[0:1, :] - e[1:2, :] + lam_init


def _flash_kernel(slope_ref, q_ref, k_ref, v_ref, lq_ref, lk_ref, swt_ref, o_ref,
                  vt_ref, s_ref, mx_ref, m_ref, l_ref, acc_ref, *, lam_init):
    h = pl.program_id(1)
    qi = pl.program_id(2)
    tq = q_ref.shape[0]
    tk = tq
    seq = k_ref.shape[0]
    dh = q_ref.shape[1] // 2
    slope = slope_ref[h]

    @pl.when(qi == 0)
    def _transpose_v():
        def body(c, carry):
            r0 = pl.multiple_of(c * tk, tk)
            vt_ref[:, pl.ds(r0, tk)] = jnp.transpose(v_ref[pl.ds(r0, tk), :].astype(F32)).astype(BF16)
            return carry
        lax.fori_loop(0, seq // tk, body, 0)

    m_ref[...] = jnp.full(m_ref.shape, NEG, F32)
    l_ref[...] = jnp.zeros(l_ref.shape, F32)
    acc_ref[...] = jnp.zeros(acc_ref.shape, F32)

    q = q_ref[...]
    kidx = lax.broadcasted_iota(jnp.int32, (tk, tq), 0)
    qidx = lax.broadcasted_iota(jnp.int32, (tk, tq), 1)
    rel = (qidx - kidx).astype(F32)

    def qk(kj, mi):
        r0 = pl.multiple_of(kj * tk, tk)
        return lax.dot_general(k_ref[pl.ds(r0, tk), mi * dh:(mi + 1) * dh], q[:, mi * dh:(mi + 1) * dh], _NT,
                               preferred_element_type=F32)

    def scores(kj, slot):
        bias = slope * rel
        for mi in range(2):
            s = qk(kj, mi) - bias
            s_ref[slot, mi] = s
            mx_ref[slot, mi] = jnp.max(s, axis=0, keepdims=True)

    def scores_diag(slot):
        visible = (kidx >> CHUNK_SHIFT) <= (qidx >> CHUNK_SHIFT)
        bias = slope * jnp.abs(rel)
        for mi in range(2):
            s = jnp.where(visible, qk(qi, mi) - bias, NEG)
            s_ref[slot, mi] = s
            mx_ref[slot, mi] = jnp.max(s, axis=0, keepdims=True)

    def accumulate(kj, slot, off):
        vt_blk = vt_ref[:, pl.ds(pl.multiple_of(kj * tk, tk), tk)]
        for mi in range(2):
            m_prev = m_ref[mi]
            m_new = jnp.maximum(m_prev, mx_ref[slot, mi] - off)
            p = jnp.exp2(s_ref[slot, mi] - (m_new + off))
            alpha = jnp.exp2(m_prev - m_new)
            l_ref[mi] = alpha * l_ref[mi] + jnp.sum(p, axis=0, keepdims=True)
            acc_ref[mi] = alpha * acc_ref[mi] + jnp.dot(vt_blk, p.astype(BF16), preferred_element_type=F32)
            m_ref[mi] = m_new

    def block_offset(kj):
        return slope * ((qi - kj) * tq).astype(F32)

    @pl.when(qi > 0)
    def _first_scores():
        scores(0, 0)

    def body(kj, carry):
        slot = kj & 1
        accumulate(kj, slot, block_offset(kj))
        scores(kj + 1, 1 - slot)
        return carry

    lax.fori_loop(0, qi - 1, body, 0)

    @pl.when(qi > 0)
    def _last_earlier_block():
        slot = (qi - 1) & 1
        accumulate(qi - 1, slot, block_offset(qi - 1))
        scores_diag(1 - slot)

    @pl.when(qi == 0)
    def _only_diag():
        scores_diag(0)

    accumulate(qi, qi & 1, 0.0)

    o = acc_ref[0] / l_ref[0] - _lam(lq_ref, lk_ref, lam_init) * (acc_ref[1] / l_ref[1])
    ms = jnp.mean(o * o, axis=0, keepdims=True)
    o = o * lax.rsqrt(ms + EPS) * swt_ref[...] * (1.0 - lam_init)
    o_ref[...] = jnp.transpose(o).astype(o_ref.dtype)


def _flash_attention(q, k, v, lam_q, lam_k, subln_w, lam_init, *, tq=512):
    bsz, seq, width = q.shape
    hw = width // ATT_HEADS
    tq = min(tq, seq)
    assert seq % tq == 0 and tq % CHUNK == 0
    smem = pl.BlockSpec(memory_space=pltpu.SMEM)
    full = lambda shape: pl.BlockSpec(shape, lambda b, h, i: (0,) * len(shape))
    swt = jnp.broadcast_to(subln_w.astype(F32)[:, None], (hw, tq))
    return pl.pallas_call(
        functools.partial(_flash_kernel, lam_init=lam_init),
        grid=(bsz, ATT_HEADS, seq // tq),
        in_specs=[smem,
                  pl.BlockSpec((None, tq, hw), lambda b, h, i: (b, i, h)),
                  pl.BlockSpec((None, seq, hw), lambda b, h, i: (b, 0, h)),
                  pl.BlockSpec((None, seq, hw), lambda b, h, i: (b, 0, h)),
                  full(lam_q.shape), full(lam_k.shape), full((hw, tq))],
        out_specs=pl.BlockSpec((None, tq, hw), lambda b, h, i: (b, i, h)),
        out_shape=jax.ShapeDtypeStruct((bsz, seq, width), BF16),
        scratch_shapes=[pltpu.VMEM((hw, seq), BF16),
                        pltpu.VMEM((2, 2, tq, tq), F32), pltpu.VMEM((2, 2, 1, tq), F32),
                        pltpu.VMEM((2, 1, tq), F32), pltpu.VMEM((2, 1, tq), F32),
                        pltpu.VMEM((2, hw, tq), F32)],
        compiler_params=_cparams(3),
        name="flash_diff_attn",
    )(jnp.asarray(_alibi_slopes() * LOG2E, F32), q, k, v, lam_q.astype(F32), lam_k.astype(F32), swt)


DECODE_PAST_BLOCK = 512


def _decode_kernel(q_ref, kn_ref, vn_ref, kc_ref, vc_ref, srow_ref, lq_ref, lk_ref, sw_ref, o_ref,
                   mb_ref, m_ref, l_ref, acc_ref, *, lam_init, past):
    pi = pl.program_id(1)
    npb = pl.num_programs(1)
    tq = q_ref.shape[0]
    rows = ATT_HEADS * tq
    cols = kc_ref.shape[0]
    pblk = cols // ATT_HEADS
    hw = sw_ref.shape[1]
    dh = hw // 2
    t_shift = tq.bit_length() - 1
    h_shift = ATT_HEADS.bit_length() - 1
    srow = srow_ref[:, 0:1]

    @pl.when(pi == 0)
    def _init():
        m_ref[...] = jnp.full(m_ref.shape, NEG, F32)
        l_ref[...] = jnp.zeros(l_ref.shape, F32)
        acc_ref[...] = jnp.zeros(acc_ref.shape, F32)
        r = lax.broadcasted_iota(jnp.int32, (rows, cols), 0)
        c = lax.broadcasted_iota(jnp.int32, (rows, cols), 1)
        same_head = (c & (ATT_HEADS - 1)) == (r >> t_shift)
        rel = ((r & (tq - 1)) - (c >> h_shift)).astype(F32)
        mb_ref[...] = jnp.where(same_head, -(srow * rel), NEG)

    def update(mi, rs, s, vblk):
        m_prev = m_ref[mi, rs]
        m_new = jnp.maximum(m_prev, jnp.max(s, axis=-1, keepdims=True))
        p = jnp.exp2(s - m_new)
        alpha = jnp.exp2(m_prev - m_new)
        l_ref[mi, rs] = alpha * l_ref[mi, rs] + jnp.sum(p, axis=-1, keepdims=True)
        acc_ref[mi, rs] = alpha * acc_ref[mi, rs] + jnp.dot(p.astype(BF16), vblk, preferred_element_type=F32)
        m_ref[mi, rs] = m_new

    off = srow * (past - pi * pblk).astype(F32)
    kc = kc_ref[...].astype(BF16)
    vc = vc_ref[...].astype(BF16)
    for mi in range(2):
        qm = jnp.concatenate([q_ref[:, h * hw + mi * dh:h * hw + (mi + 1) * dh] for h in range(ATT_HEADS)], axis=0)
        s = lax.dot_general(qm, kc[:, mi * dh:(mi + 1) * dh], _NT, preferred_element_type=F32)
        update(mi, slice(None), s + mb_ref[...] - off, vc)

    @pl.when(pi == npb - 1)
    def _finish():
        slopes = _alibi_slopes() * LOG2E
        ri = lax.broadcasted_iota(jnp.int32, (tq, tq), 0)
        ci = lax.broadcasted_iota(jnp.int32, (tq, tq), 1)
        dist_n = jnp.abs(ri - ci).astype(F32)
        vis_n = ((past + ci) >> CHUNK_SHIFT) <= ((past + ri) >> CHUNK_SHIFT)
        lam = _lam(lq_ref, lk_ref, lam_init)
        for h in range(ATT_HEADS):
            hs = slice(h * hw, (h + 1) * hw)
            rs = slice(h * tq, (h + 1) * tq)
            kn = kn_ref[:, hs].astype(BF16)
            vn = vn_ref[:, hs].astype(BF16)
            bias = float(slopes[h]) * dist_n
            for mi in range(2):
                qm = q_ref[:, h * hw + mi * dh:h * hw + (mi + 1) * dh]
                s = lax.dot_general(qm, kn[:, mi * dh:(mi + 1) * dh], _NT, preferred_element_type=F32) - bias
                update(mi, rs, jnp.where(vis_n, s, NEG), vn)
            o = acc_ref[0, rs] / l_ref[0, rs] - lam * (acc_ref[1, rs] / l_ref[1, rs])
            ms = jnp.mean(o * o, axis=-1, keepdims=True)
            o_ref[:, hs] = (o * lax.rsqrt(ms + EPS) * sw_ref[...] * (1.0 - lam_init)).astype(o_ref.dtype)


def _decode_attention(q, k_new, v_new, cache_k, cache_v, layer, lam_q, lam_k, subln_w, lam_init):
    bsz, tq, width = q.shape
    hw = width // ATT_HEADS
    past = cache_k.shape[2] // ATT_HEADS
    pblk = min(past, DECODE_PAST_BLOCK)
    rows = ATT_HEADS * tq
    assert past % pblk == 0 and tq & (tq - 1) == 0 and ATT_HEADS & (ATT_HEADS - 1) == 0
    srow = np.repeat(_alibi_slopes() * LOG2E, tq)[:, None] * np.ones((1, V7X_LANES))
    full = lambda shape: pl.BlockSpec(shape, lambda b, p: (0,) * len(shape))
    new = pl.BlockSpec((None, tq, width), lambda b, p: (b, 0, 0))
    cache = pl.BlockSpec((None, None, pblk * ATT_HEADS, hw), lambda b, p: (layer, b, p, 0))
    return pl.pallas_call(
        functools.partial(_decode_kernel, lam_init=lam_init, past=past),
        grid=(bsz, past // pblk),
        in_specs=[new, new, new, cache, cache, full((rows, V7X_LANES)), full(lam_q.shape), full(lam_k.shape),
                  full((1, hw))],
        out_specs=new,
        out_shape=jax.ShapeDtypeStruct((bsz, tq, width), BF16),
        scratch_shapes=[pltpu.VMEM((rows, pblk * ATT_HEADS), F32),
                        pltpu.VMEM((2, rows, 1), F32), pltpu.VMEM((2, rows, 1), F32),
                        pltpu.VMEM((2, rows, hw), F32)],
        compiler_params=_cparams(2),
        name="decode_diff_attn",
    )(q, k_new, v_new, cache_k, cache_v, jnp.asarray(srow, F32), lam_q.astype(F32), lam_k.astype(F32),
      subln_w.astype(F32).reshape(1, hw))


QKV_TN = 512
SSD_IN_TN = 1024
SSD_CHUNK = 128


def _ssd_layer(h, bsz, norm_w, w_in, w_dt, conv0, h0, conv_w, conv_b, dt_bias, a_log, d_skip, gnorm_w, w_out,
               d_inner, conv_dim):
    m = h.shape[0]
    seq = m // bsz
    z, xbc, dt = _norm_mm(
        h, norm_w, w_in,
        [(d_inner // SSD_IN_TN, [(BF16, 1.0)]), (conv_dim // SSD_IN_TN, [(BF16, 1.0)])],
        tn=SSD_IN_TN, w_side=w_dt)
    g, conv_new, h_new = _ssd_scan(
        z.reshape(bsz, seq, d_inner), xbc.reshape(bsz, seq, conv_dim), dt.reshape(bsz, seq, V7X_LANES),
        conv0, h0, conv_w, conv_b, dt_bias, a_log, d_skip, gnorm_w, chunk=SSD_CHUNK)
    return _mm_res(g.reshape(m, d_inner), w_out, h), conv_new, h_new


def _qkv(h, norm_w, w_qkv, width):
    dh = width // (2 * ATT_HEADS)
    nt = width // QKV_TN
    kv = [(BF16, 1.0), (F32, 1.0)]
    return _norm_mm(h, norm_w, w_qkv, [(nt, [(BF16, dh ** -0.5 * LOG2E)]), (nt, kv), (nt, kv)], tn=QKV_TN)


def kernel(x_prompt, x_sample, cache_k, cache_v, state_ssm, state_conv, norm_mix_w, norm_mlp_w, final_norm_w, ssd_w_in, ssd_conv_w, ssd_conv_b, ssd_dt_bias, ssd_a_log, ssd_d, ssd_norm_w, ssd_w_out, att_w_qkv, att_lam_q, att_lam_k, att_subln_w, att_w_o, mlp_w_up, mlp_w_down):
    bp, sp, d_model = x_prompt.shape
    bs, ss, _ = x_sample.shape
    depth = norm_mix_w.shape[0]
    d_inner = ssd_w_out.shape[1]
    conv_dim = ssd_conv_w.shape[2]
    n_heads = ssd_d.shape[1]
    att_width = att_w_o.shape[1]
    hw = att_width // ATT_HEADS

    hp = x_prompt.reshape(bp * sp, d_model)
    hs = x_sample.reshape(bs * ss, d_model)
    cache_k = cache_k.reshape(cache_k.shape[0], bs, -1, hw)
    cache_v = cache_v.reshape(cache_v.shape[0], bs, -1, hw)

    k_p, v_p, ssm_p, conv_p, k_s, v_s, ssm_s, conv_s = [], [], [], [], [], [], [], []
    for i in range(depth):
        j = i // 2
        if i % 2 == 0:
            w_in = ssd_w_in[j][:, :d_inner + conv_dim].astype(BF16)
            w_dt = jnp.pad(ssd_w_in[j][:, d_inner + conv_dim:], ((0, 0), (0, V7X_LANES - n_heads))).astype(BF16)
            w_out = ssd_w_out[j].astype(BF16)
            args = (ssd_conv_w[j], ssd_conv_b[j], ssd_dt_bias[j], ssd_a_log[j], ssd_d[j], ssd_norm_w[j], w_out,
                    d_inner, conv_dim)
            hp, cp, sp_state = _ssd_layer(
                hp, bp, norm_mix_w[i], w_in, w_dt, jnp.zeros((bp, D_CONV - 1, conv_dim), F32),
                jnp.zeros((bp, n_heads, SSD_HEAD_DIM, D_STATE), F32), *args)
            hs, cs, ss_state = _ssd_layer(hs, bs, norm_mix_w[i], w_in, w_dt, state_conv[j], state_ssm[j], *args)
            conv_p.append(cp)
            ssm_p.append(sp_state)
            conv_s.append(cs)
            ssm_s.append(ss_state)
        else:
            w_qkv = att_w_qkv[j].astype(BF16)
            w_o = att_w_o[j].astype(BF16)
            lam_init = 0.8 - 0.6 * math.exp(-0.3 * i)
            qp, kpb, kp, vpb, vp = _qkv(hp, norm_mix_w[i], w_qkv, att_width)
            op = _flash_attention(qp.reshape(bp, sp, att_width), kpb.reshape(bp, sp, att_width),
                                  vpb.reshape(bp, sp, att_width), att_lam_q[j], att_lam_k[j], att_subln_w[j], lam_init)
            hp = _mm_res(op.reshape(bp * sp, att_width), w_o, hp)
            qs, _, ks, _, vs = _qkv(hs, norm_mix_w[i], w_qkv, att_width)
            os_ = _decode_attention(qs.reshape(bs, ss, att_width), ks.reshape(bs, ss, att_width),
                                    vs.reshape(bs, ss, att_width), cache_k, cache_v, j,
                                    att_lam_q[j], att_lam_k[j], att_subln_w[j], lam_init)
            hs = _mm_res(os_.reshape(bs * ss, att_width), w_o, hs)
            k_p.append(kp.reshape(bp, sp, ATT_HEADS, hw))
            v_p.append(vp.reshape(bp, sp, ATT_HEADS, hw))
            k_s.append(ks.reshape(bs, ss, ATT_HEADS, hw))
            v_s.append(vs.reshape(bs, ss, ATT_HEADS, hw))
        w_up = mlp_w_up[i].astype(BF16)
        w_down = mlp_w_down[i].astype(BF16)
        final_w = final_norm_w if i == depth - 1 else None
        hp = _mlp(hp, norm_mlp_w[i], w_up, w_down, final_w)
        hs = _mlp(hs, norm_mlp_w[i], w_up, w_down, final_w)

    y_prompt = hp.reshape(bp, sp, d_model)
    y_sample = hs.reshape(bs, ss, d_model)
    return (y_prompt, y_sample, jnp.stack(k_p), jnp.stack(v_p), jnp.stack(ssm_p), jnp.stack(conv_p),
            jnp.stack(k_s), jnp.stack(v_s), jnp.stack(ssm_s), jnp.stack(conv_s))
```

```python
import functools
import math

import numpy as np
import jax
import jax.numpy as jnp
from jax import lax
from jax.experimental import pallas as pl
from jax.experimental.pallas import tpu as pltpu

F32 = jnp.float32
BF16 = jnp.bfloat16

EPS = 1e-5
CHUNK = 64
CHUNK_SHIFT = 6
D_CONV = 4
SSD_HEAD_DIM = 64
SSD_GROUPS = 8
D_STATE = 128
ATT_HEADS = 8
NEG = -1e30
LOG2E = math.log2(math.e)

V7X_LANES = 128
V7X_SUBLANES = 8
V7X_BF16_ROWS = 16
V7X_VMEM_LIMIT_BYTES = 56 * 1024 * 1024

_NT = (((1,), (1,)), ((), ()))


def _cparams(n_axes):
    return pltpu.CompilerParams(dimension_semantics=("arbitrary",) * n_axes,
                                vmem_limit_bytes=V7X_VMEM_LIMIT_BYTES)


def _silu(u):
    half = 0.5 * u
    return half * jnp.tanh(half) + half


def _softplus(u):
    return jnp.maximum(u, 0.0) + jnp.log(1.0 + jnp.exp(-jnp.abs(u)))


def _split3(x):
    hi = x.astype(BF16)
    r = x - hi.astype(F32)
    mid = r.astype(BF16)
    lo = (r - mid.astype(F32)).astype(BF16)
    return hi, mid, lo


def _dot01(sel, x):
    hi, mid, lo = _split3(x)
    return (jnp.dot(sel, hi, preferred_element_type=F32) + jnp.dot(sel, mid, preferred_element_type=F32)
            + jnp.dot(sel, lo, preferred_element_type=F32))


def _normalise_rows(x_ref, nw_ref, xn_ref):
    nw = nw_ref[...]

    def body(c, carry):
        r0 = pl.multiple_of(c * NORM_ROWS, NORM_ROWS)
        x = x_ref[pl.ds(r0, NORM_ROWS), :]
        ms = jnp.mean(x * x, axis=-1, keepdims=True)
        xn_ref[pl.ds(r0, NORM_ROWS), :] = (x * lax.rsqrt(ms + EPS) * nw).astype(xn_ref.dtype)
        return carry

    lax.fori_loop(0, x_ref.shape[0] // NORM_ROWS, body, 0, unroll=NORM_UNROLL)


NORM_ROWS = 16
NORM_UNROLL = 8


def _norm_mm_kernel(*refs, seg_lo, seg_hi, seg_outs, has_side):
    n_in = 4 if has_side else 3
    x_ref, nw_ref, w_ref = refs[:3]
    out_refs, xn_ref = refs[n_in:-1], refs[-1]
    j = pl.program_id(1)

    @pl.when(j == 0)
    def _normalise():
        _normalise_rows(x_ref, nw_ref, xn_ref)
        if has_side:
            out_refs[-1][...] = jnp.dot(xn_ref[...], refs[3][...], preferred_element_type=F32)

    acc = jnp.dot(xn_ref[...], w_ref[...], preferred_element_type=F32)

    k = 0
    for s, outs in enumerate(seg_outs):
        seg_refs = out_refs[k:k + len(outs)]
        k += len(outs)

        def write(seg_refs=seg_refs, outs=outs):
            for o_ref, scale in zip(seg_refs, outs):
                o_ref[...] = (acc if scale == 1.0 else acc * scale).astype(o_ref.dtype)

        if len(seg_outs) == 1:
            write()
        else:
            pl.when((j >= seg_lo[s]) & (j < seg_hi[s]))(write)


def _norm_mm(x, nw, w, segs, *, tn, w_side=None):
    m, kdim = x.shape
    n = w.shape[1]
    tm = min(m, 1024)
    assert m % tm == 0 and n % tn == 0 and tm % NORM_ROWS == 0
    lo, seg_lo, seg_hi = 0, [], []
    for nt, _ in segs:
        seg_lo.append(lo)
        lo += nt
        seg_hi.append(lo)
    assert lo == n // tn
    out_shape, out_specs = [], []
    for (nt, outs), s_lo in zip(segs, seg_lo):
        for dtype, _ in outs:
            out_shape.append(jax.ShapeDtypeStruct((m, nt * tn), dtype))
            out_specs.append(pl.BlockSpec(
                (tm, tn), lambda i, j, s_lo=s_lo, nt=nt: (i, jnp.clip(j - s_lo, 0, nt - 1))))
    in_specs = [pl.BlockSpec((tm, kdim), lambda i, j: (i, 0)),
                pl.BlockSpec((1, kdim), lambda i, j: (0, 0)),
                pl.BlockSpec((kdim, tn), lambda i, j: (0, j))]
    args = [x, nw.reshape(1, kdim), w]
    if w_side is not None:
        in_specs.append(pl.BlockSpec(w_side.shape, lambda i, j: (0, 0)))
        args.append(w_side)
        out_shape.append(jax.ShapeDtypeStruct((m, w_side.shape[1]), F32))
        out_specs.append(pl.BlockSpec((tm, w_side.shape[1]), lambda i, j: (i, 0)))
    kern = functools.partial(
        _norm_mm_kernel, seg_lo=tuple(seg_lo), seg_hi=tuple(seg_hi),
        seg_outs=tuple(tuple(float(sc) for _, sc in s[1]) for s in segs), has_side=w_side is not None)
    return pl.pallas_call(
        kern,
        grid=(m // tm, n // tn),
        in_specs=in_specs,
        out_specs=out_specs,
        out_shape=out_shape,
        scratch_shapes=[pltpu.VMEM((tm, kdim), BF16)],
        compiler_params=_cparams(2),
        name="norm_mm",
    )(*args)


def _mm_res_kernel(a_ref, w_ref, r_ref, o_ref):
    o_ref[...] = r_ref[...] + jnp.dot(a_ref[...], w_ref[...], preferred_element_type=F32)


MM_RES_LHS_BLOCK_BYTES = 2 * 1024 * 1024


def _mm_res(a, w, res):
    m, kdim = a.shape
    n = w.shape[1]
    tm = min(m, MM_RES_LHS_BLOCK_BYTES // (kdim * a.dtype.itemsize))
    assert m % tm == 0
    return pl.pallas_call(
        _mm_res_kernel,
        grid=(m // tm,),
        in_specs=[pl.BlockSpec((tm, kdim), lambda i: (i, 0)),
                  pl.BlockSpec((kdim, n), lambda i: (0, 0), pipeline_mode=pl.Buffered(1)),
                  pl.BlockSpec((tm, n), lambda i: (i, 0))],
        out_specs=pl.BlockSpec((tm, n), lambda i: (i, 0)),
        out_shape=jax.ShapeDtypeStruct((m, n), F32),
        compiler_params=_cparams(1),
        name="mm_res",
    )(a, w, res)


def _mlp_kernel(x_ref, nw_ref, wu_ref, wd_ref, *rest, final_norm):
    fw_ref = rest[0] if final_norm else None
    o_ref, xn_ref = rest[-2], rest[-1]
    j = pl.program_id(1)

    @pl.when(j == 0)
    def _start():
        _normalise_rows(x_ref, nw_ref, xn_ref)
        o_ref[...] = x_ref[...]

    h = jnp.dot(xn_ref[...], wu_ref[...], preferred_element_type=F32)
    h = jnp.square(jnp.maximum(h, 0.0)).astype(BF16)
    o_ref[...] += jnp.dot(h, wd_ref[...], preferred_element_type=F32)

    if final_norm:
        @pl.when(j == pl.num_programs(1) - 1)
        def _final_norm():
            _normalise_rows(o_ref, fw_ref, o_ref)


def _mlp(x, nw, w_up, w_down, final_w=None, *, tf=512):
    m, d = x.shape
    d_ff = w_up.shape[1]
    tm = min(m, 1024)
    assert m % tm == 0 and d_ff % tf == 0
    in_specs = [pl.BlockSpec((tm, d), lambda i, j: (i, 0)),
                pl.BlockSpec((1, d), lambda i, j: (0, 0)),
                pl.BlockSpec((d, tf), lambda i, j: (0, j)),
                pl.BlockSpec((tf, d), lambda i, j: (j, 0))]
    args = [x, nw.reshape(1, d), w_up, w_down]
    if final_w is not None:
        in_specs.append(pl.BlockSpec((1, d), lambda i, j: (0, 0)))
        args.append(final_w.reshape(1, d))
    return pl.pallas_call(
        functools.partial(_mlp_kernel, final_norm=final_w is not None),
        grid=(m // tm, d_ff // tf),
        in_specs=in_specs,
        out_specs=pl.BlockSpec((tm, d), lambda i, j: (i, 0)),
        out_shape=jax.ShapeDtypeStruct((m, d), F32),
        scratch_shapes=[pltpu.VMEM((tm, d), BF16)],
        compiler_params=_cparams(2),
        name="mlp",
    )(*args)


CONV_SLAB = 512
CONV_UNROLL = 4
HALO = V7X_BF16_ROWS


def _ssd_kernel(z_ref, xbc_ref, dt_ref, conv0_ref, h0_ref, cw_ref, cb_ref, dtb_ref, alog_ref,
                dsk_ref, nw_ref, e_ref, shift_ref, g_ref, convo_ref, ho_ref, xp_ref, ht_ref, xs_ref, bc_ref):
    t = pl.program_id(1)
    nt = pl.num_programs(1)
    L = z_ref.shape[0]
    d_inner = z_ref.shape[1]
    n_bc = bc_ref.shape[1]
    gw = d_inner // SSD_GROUPS
    hpg = gw // SSD_HEAD_DIM

    eye_r = lax.broadcasted_iota(jnp.int32, (V7X_LANES, V7X_LANES), 0)
    eye_c = lax.broadcasted_iota(jnp.int32, (V7X_LANES, V7X_LANES), 1)
    eye_b = (eye_r == eye_c).astype(F32).astype(BF16)

    @pl.when(t == 0)
    def _init():
        xp_ref[0:HALO, :] = conv0_ref[...]
        for g in range(SSD_GROUPS):
            ht_ref[g] = jnp.transpose(h0_ref[g])

    xp_ref[HALO:HALO + L, :] = xbc_ref[...]

    def conv_slab(c0):
        taps = jnp.dot(shift_ref[...], xp_ref[:, pl.ds(c0, CONV_SLAB)], preferred_element_type=F32)
        u = cb_ref[:, pl.ds(c0, CONV_SLAB)]
        for k in range(D_CONV):
            u = u + cw_ref[pl.ds(k, 1), pl.ds(c0, CONV_SLAB)] * taps[k * L:(k + 1) * L]
        return _silu(u)

    def x_body(c, carry):
        c0 = pl.multiple_of(c * CONV_SLAB, CONV_SLAB)
        xs_ref[:, pl.ds(c0, CONV_SLAB)] = conv_slab(c0)
        return carry

    lax.fori_loop(0, d_inner // CONV_SLAB, x_body, 0, unroll=CONV_UNROLL)

    def bc_body(c, carry):
        c0 = pl.multiple_of(c * CONV_SLAB, CONV_SLAB)
        bc_ref[:, pl.ds(c0, CONV_SLAB)] = conv_slab(d_inner + c0).astype(BF16)
        return carry

    lax.fori_loop(0, n_bc // CONV_SLAB, bc_body, 0, unroll=CONV_UNROLL)

    dt = _softplus(dt_ref[...] + dtb_ref[...])
    a = dt * (-jnp.exp(alog_ref[...]))
    ri = lax.broadcasted_iota(jnp.int32, (L, L), 0)
    ci = lax.broadcasted_iota(jnp.int32, (L, L), 1)
    tri = ri >= ci
    acum = _dot01(tri.astype(F32).astype(BF16), a)
    a_hi, a_mid, a_lo = _split3(acum)
    acum_t = (lax.dot_general(eye_b, a_hi, _NT, preferred_element_type=F32)
              + lax.dot_general(eye_b, a_mid, _NT, preferred_element_type=F32)
              + lax.dot_general(eye_b, a_lo, _NT, preferred_element_type=F32))
    last = acum[L - 1:L, :]
    per_head = jnp.concatenate(
        [dt.astype(BF16), jnp.exp(last - acum).astype(BF16), jnp.exp(acum).astype(BF16)], axis=0)
    cd8 =jnp.broadcast_to(jnp.exp(last), (V7X_SUBLANES, V7X_LANES))
    c_hi, c_mid, c_lo = _split3(cd8)
    lane = lax.broadcasted_iota(jnp.int32, (L, V7X_LANES), 1)
    lane_lo = lane < SSD_HEAD_DIM

    for g in range(SSD_GROUPS):
        cs = slice(g * gw, (g + 1) * gw)
        e_g = e_ref[:, cs]
        ex = jnp.dot(per_head, e_g, preferred_element_type=F32)
        dtx, tlx, eax = ex[0:L], ex[L:2 * L], ex[2 * L:3 * L]
        cdx = (jnp.dot(c_hi, e_g, preferred_element_type=F32) + jnp.dot(c_mid, e_g, preferred_element_type=F32)
               + jnp.dot(c_lo, e_g, preferred_element_type=F32))[0:1, :]
        xs_g = xs_ref[:, cs]
        xdt = xs_g * dtx
        b_g = bc_ref[:, g * D_STATE:(g + 1) * D_STATE]
        c_g = bc_ref[:, n_bc // 2 + g * D_STATE:n_bc // 2 + (g + 1) * D_STATE]
        cb = lax.dot_general(c_g, b_g, _NT, preferred_element_type=F32)
        ht = ht_ref[g]
        y_state = jnp.dot(c_g, ht.astype(BF16), preferred_element_type=F32) * eax
        b_gt = lax.dot_general(eye_b, b_g, _NT, preferred_element_type=F32).astype(BF16)
        ht_ref[g] = ht * cdx + jnp.dot(b_gt, (xdt * tlx).astype(BF16), preferred_element_type=F32)

        pairs = []
        for p in range(gw // V7X_LANES):
            ls = slice(p * V7X_LANES, (p + 1) * V7X_LANES)
            xdt_p = xdt[:, ls]
            y_p = y_state[:, ls] + xs_g[:, ls] * dsk_ref[:, g * gw + p * V7X_LANES:g * gw + (p + 1) * V7X_LANES]
            for q in range(V7X_LANES // SSD_HEAD_DIM):
                h = g * hpg + p * (V7X_LANES // SSD_HEAD_DIM) + q
                seg = acum[:, h:h + 1] - acum_t[h:h + 1, :]
                m = (jnp.exp(jnp.where(tri, seg, NEG)) * cb).astype(BF16)
                rhs = jnp.where(lane_lo if q == 0 else jnp.logical_not(lane_lo), xdt_p, 0.0).astype(BF16)
                y_p = y_p + jnp.dot(m, rhs, preferred_element_type=F32)
            pairs.append(y_p)
        y_g = jnp.concatenate(pairs, axis=1)
        gate = y_g * _silu(z_ref[:, cs].astype(F32))
        ms = jnp.mean(gate * gate, axis=-1, keepdims=True)
        g_ref[:, cs] = (gate * lax.rsqrt(ms + EPS) * nw_ref[:, cs]).astype(g_ref.dtype)

    xp_ref[0:HALO, :] = xp_ref[L:L + HALO, :]

    @pl.when(t == nt - 1)
    def _finish():
        convo_ref[...] = xp_ref[0:HALO, :].astype(F32)[HALO - (D_CONV - 1):HALO, :]
        for g in range(SSD_GROUPS):
            ho_ref[g] = jnp.transpose(ht_ref[g])


def _ssd_scan(z, xbc, dt, conv0, h0, conv_w, conv_b, dt_bias, a_log, d_skip, norm_w, *, chunk):
    bsz, seq, d_inner = z.shape
    conv_dim = xbc.shape[2]
    n_heads = d_inner // SSD_HEAD_DIM
    gw = d_inner // SSD_GROUPS
    L = min(chunk, seq)
    assert seq % L == 0 and n_heads <= V7X_LANES and L % HALO == 0
    pad = V7X_LANES - n_heads
    expand = np.zeros((V7X_LANES, d_inner), np.float32)
    expand[np.arange(d_inner) // SSD_HEAD_DIM, np.arange(d_inner)] = 1.0
    shift = np.zeros((D_CONV * L, L + HALO), np.float32)
    for k in range(D_CONV):
        shift[k * L + np.arange(L), HALO - (D_CONV - 1) + k + np.arange(L)] = 1.0
    conv0_p = jnp.pad(conv0.astype(BF16), ((0, 0), (HALO - (D_CONV - 1), 0), (0, 0)))
    row = lambda v: v.astype(F32).reshape(1, -1)
    full = lambda shape: pl.BlockSpec(shape, lambda b, t: (0,) * len(shape))
    g_out, conv_out, h_out = pl.pallas_call(
        _ssd_kernel,
        grid=(bsz, seq // L),
        in_specs=[pl.BlockSpec((None, L, d_inner), lambda b, t: (b, t, 0)),
                  pl.BlockSpec((None, L, conv_dim), lambda b, t: (b, t, 0)),
                  pl.BlockSpec((None, L, V7X_LANES), lambda b, t: (b, t, 0)),
                  pl.BlockSpec((None, HALO, conv_dim), lambda b, t: (b, 0, 0)),
                  pl.BlockSpec((None, SSD_GROUPS, gw, D_STATE), lambda b, t: (b, 0, 0, 0)),
                  full((D_CONV, conv_dim)), full((1, conv_dim)), full((1, V7X_LANES)), full((1, V7X_LANES)),
                  full((1, d_inner)), full((1, d_inner)), full((V7X_LANES, d_inner)),
                  full((D_CONV * L, L + HALO))],
        out_specs=[pl.BlockSpec((None, L, d_inner), lambda b, t: (b, t, 0)),
                   pl.BlockSpec((None, D_CONV - 1, conv_dim), lambda b, t: (b, 0, 0)),
                   pl.BlockSpec((None, SSD_GROUPS, gw, D_STATE), lambda b, t: (b, 0, 0, 0))],
        out_shape=[jax.ShapeDtypeStruct((bsz, seq, d_inner), BF16),
                   jax.ShapeDtypeStruct((bsz, D_CONV - 1, conv_dim), F32),
                   jax.ShapeDtypeStruct((bsz, SSD_GROUPS, gw, D_STATE), F32)],
        scratch_shapes=[pltpu.VMEM((L + HALO, conv_dim), BF16),
                        pltpu.VMEM((SSD_GROUPS, D_STATE, gw), F32),
                        pltpu.VMEM((L, d_inner), F32),
                        pltpu.VMEM((L, conv_dim - d_inner), BF16)],
        compiler_params=_cparams(2),
        name="ssd_scan",
    )(z, xbc, dt, conv0_p, h0.astype(F32).reshape(bsz, SSD_GROUPS, gw, D_STATE),
      conv_w.astype(F32), row(conv_b), jnp.pad(row(dt_bias), ((0, 0), (0, pad))),
      jnp.pad(row(a_log), ((0, 0), (0, pad))), row(jnp.repeat(d_skip, SSD_HEAD_DIM)), row(norm_w),
      jnp.asarray(expand, BF16), jnp.asarray(shift, BF16))
    return g_out, conv_out, h_out.reshape(bsz, n_heads, SSD_HEAD_DIM, D_STATE)


def _alibi_slopes():
    return 2.0 ** (-8.0 * np.arange(1, ATT_HEADS + 1, dtype=np.float64) / ATT_HEADS)


def _lam(lq_ref, lk_ref, lam_init):
    e = jnp.exp(jnp.sum(lq_ref[...] * lk_ref[...], axis=-1, keepdims=True))
    return e[0:1, :] - e[1:2, :] + lam_init


def _flash_kernel(slope_ref, q_ref, k_ref, v_ref, lq_ref, lk_ref, swt_ref, o_ref,
                  kb_ref, vt_ref, s_ref, mx_ref, m_ref, l_ref, acc_ref, *, lam_init):
    h = pl.program_id(1)
    qi = pl.program_id(2)
    tq = q_ref.shape[0]
    tk = tq
    seq = k_ref.shape[0]
    dh = q_ref.shape[1] // 2
    slope = slope_ref[h]

    @pl.when(qi == 0)
    def _stage_kv():
        def body(c, carry):
            r0 = pl.multiple_of(c * tk, tk)
            kb_ref[pl.ds(r0, tk), :] = k_ref[pl.ds(r0, tk), :].astype(BF16)
            vt_ref[:, pl.ds(r0, tk)] = jnp.transpose(v_ref[pl.ds(r0, tk), :]).astype(BF16)
            return carry
        lax.fori_loop(0, seq // tk, body, 0)

    m_ref[...] = jnp.full(m_ref.shape, NEG, F32)
    l_ref[...] = jnp.zeros(l_ref.shape, F32)
    acc_ref[...] = jnp.zeros(acc_ref.shape, F32)

    q = q_ref[...]
    kidx = lax.broadcasted_iota(jnp.int32, (tk, tq), 0)
    qidx = lax.broadcasted_iota(jnp.int32, (tk, tq), 1)
    rel = (qidx - kidx).astype(F32)

    def qk(kj, mi):
        r0 = pl.multiple_of(kj * tk, tk)
        return lax.dot_general(kb_ref[pl.ds(r0, tk), mi * dh:(mi + 1) * dh], q[:, mi * dh:(mi + 1) * dh], _NT,
                               preferred_element_type=F32)

    def scores(kj, slot):
        bias = slope * rel
        for mi in range(2):
            s = qk(kj, mi) - bias
            s_ref[slot, mi] = s
            mx_ref[slot, mi] = jnp.max(s, axis=0, keepdims=True)

    def scores_diag(slot):
        visible = (kidx >> CHUNK_SHIFT) <= (qidx >> CHUNK_SHIFT)
        bias = slope * jnp.abs(rel)
        for mi in range(2):
            s = jnp.where(visible, qk(qi, mi) - bias, NEG)
            s_ref[slot, mi] = s
            mx_ref[slot, mi] = jnp.max(s, axis=0, keepdims=True)

    def accumulate(kj, slot, off):
        vt_blk = vt_ref[:, pl.ds(pl.multiple_of(kj * tk, tk), tk)]
        for mi in range(2):
            m_prev = m_ref[mi]
            m_new = jnp.maximum(m_prev, mx_ref[slot, mi] - off)
            p = jnp.exp2(s_ref[slot, mi] - (m_new + off))
            alpha = jnp.exp2(m_prev - m_new)
            l_ref[mi] = alpha * l_ref[mi] + jnp.sum(p, axis=0, keepdims=True)
            acc_ref[mi] = alpha * acc_ref[mi] + jnp.dot(vt_blk, p.astype(BF16), preferred_element_type=F32)
            m_ref[mi] = m_new

    def block_offset(kj):
        return slope * ((qi - kj) * tq).astype(F32)

    @pl.when(qi > 0)
    def _first_scores():
        scores(0, 0)

    def body(kj, carry):
        slot = kj & 1
        accumulate(kj, slot, block_offset(kj))
        scores(kj + 1, 1 - slot)
        return carry

    lax.fori_loop(0, qi - 1, body, 0)

    @pl.when(qi > 0)
    def _last_earlier_block():
        slot = (qi - 1) & 1
        accumulate(qi - 1, slot, block_offset(qi - 1))
        scores_diag(1 - slot)

    @pl.when(qi == 0)
    def _only_diag():
        scores_diag(0)

    accumulate(qi, qi & 1, 0.0)

    o = acc_ref[0] / l_ref[0] - _lam(lq_ref, lk_ref, lam_init) * (acc_ref[1] / l_ref[1])
    ms = jnp.mean(o * o, axis=0, keepdims=True)
    o = o * lax.rsqrt(ms + EPS) * swt_ref[...] * (1.0 - lam_init)
    o_ref[...] = jnp.transpose(o).astype(o_ref.dtype)


def _flash_attention(q, k, v, lam_q, lam_k, subln_w, lam_init, *, tq=512):
    bsz, seq, width = q.shape
    hw = width // ATT_HEADS
    tq = min(tq, seq)
    assert seq % tq == 0 and tq % CHUNK == 0
    smem = pl.BlockSpec(memory_space=pltpu.SMEM)
    full = lambda shape: pl.BlockSpec(shape, lambda b, h, i: (0,) * len(shape))
    swt = jnp.broadcast_to(subln_w.astype(F32)[:, None], (hw, tq))
    return pl.pallas_call(
        functools.partial(_flash_kernel, lam_init=lam_init),
        grid=(bsz, ATT_HEADS, seq // tq),
        in_specs=[smem,
                  pl.BlockSpec((None, tq, hw), lambda b, h, i: (b, i, h)),
                  pl.BlockSpec((None, seq, hw), lambda b, h, i: (b, 0, h)),
                  pl.BlockSpec((None, seq, hw), lambda b, h, i: (b, 0, h)),
                  full(lam_q.shape), full(lam_k.shape), full((hw, tq))],
        out_specs=pl.BlockSpec((None, tq, hw), lambda b, h, i: (b, i, h)),
        out_shape=jax.ShapeDtypeStruct((bsz, seq, width), BF16),
        scratch_shapes=[pltpu.VMEM((seq, hw), BF16), pltpu.VMEM((hw, seq), BF16),
                        pltpu.VMEM((2, 2, tq, tq), F32), pltpu.VMEM((2, 2, 1, tq), F32),
                        pltpu.VMEM((2, 1, tq), F32), pltpu.VMEM((2, 1, tq), F32),
                        pltpu.VMEM((2, hw, tq), F32)],
        compiler_params=_cparams(3),
        name="flash_diff_attn",
    )(jnp.asarray(_alibi_slopes() * LOG2E, F32), q, k, v, lam_q.astype(F32), lam_k.astype(F32), swt)


DECODE_PAST_BLOCK = 512


def _decode_kernel(q_ref, kn_ref, vn_ref, kc_ref, vc_ref, srow_ref, lq_ref, lk_ref, sw_ref, o_ref,
                   mb_ref, m_ref, l_ref, acc_ref, *, lam_init, past):
    pi = pl.program_id(1)
    npb = pl.num_programs(1)
    tq = q_ref.shape[0]
    rows = ATT_HEADS * tq
    cols = kc_ref.shape[0]
    pblk = cols // ATT_HEADS
    hw = sw_ref.shape[1]
    dh = hw // 2
    t_shift = tq.bit_length() - 1
    h_shift = ATT_HEADS.bit_length() - 1
    srow = srow_ref[:, 0:1]

    @pl.when(pi == 0)
    def _init():
        m_ref[...] = jnp.full(m_ref.shape, NEG, F32)
        l_ref[...] = jnp.zeros(l_ref.shape, F32)
        acc_ref[...] = jnp.zeros(acc_ref.shape, F32)
        r = lax.broadcasted_iota(jnp.int32, (rows, cols), 0)
        c = lax.broadcasted_iota(jnp.int32, (rows, cols), 1)
        same_head = (c & (ATT_HEADS - 1)) == (r >> t_shift)
        rel = ((r & (tq - 1)) - (c >> h_shift)).astype(F32)
        mb_ref[...] = jnp.where(same_head, -(srow * rel), NEG)

    def update(mi, rs, s, vblk):
        m_prev = m_ref[mi, rs]
        m_new = jnp.maximum(m_prev, jnp.max(s, axis=-1, keepdims=True))
        p = jnp.exp2(s - m_new)
        alpha = jnp.exp2(m_prev - m_new)
        l_ref[mi, rs] = alpha * l_ref[mi, rs] + jnp.sum(p, axis=-1, keepdims=True)
        acc_ref[mi, rs] = alpha * acc_ref[mi, rs] + jnp.dot(p.astype(BF16), vblk, preferred_element_type=F32)
        m_ref[mi, rs] = m_new

    off = srow * (past - pi * pblk).astype(F32)
    kc = kc_ref[...].astype(BF16)
    vc = vc_ref[...].astype(BF16)
    for mi in range(2):
        qm = jnp.concatenate([q_ref[:, h * hw + mi * dh:h * hw + (mi + 1) * dh] for h in range(ATT_HEADS)], axis=0)
        s = lax.dot_general(qm, kc[:, mi * dh:(mi + 1) * dh], _NT, preferred_element_type=F32)
        update(mi, slice(None), s + mb_ref[...] - off, vc)

    @pl.when(pi == npb - 1)
    def _finish():
        slopes = _alibi_slopes() * LOG2E
        ri = lax.broadcasted_iota(jnp.int32, (tq, tq), 0)
        ci = lax.broadcasted_iota(jnp.int32, (tq, tq), 1)
        dist_n = jnp.abs(ri - ci).astype(F32)
        vis_n = ((past + ci) >> CHUNK_SHIFT) <= ((past + ri) >> CHUNK_SHIFT)
        lam = _lam(lq_ref, lk_ref, lam_init)
        for h in range(ATT_HEADS):
            hs = slice(h * hw, (h + 1) * hw)
            rs = slice(h * tq, (h + 1) * tq)
            kn = kn_ref[:, hs].astype(BF16)
            vn = vn_ref[:, hs].astype(BF16)
            bias = float(slopes[h]) * dist_n
            for mi in range(2):
                qm = q_ref[:, h * hw + mi * dh:h * hw + (mi + 1) * dh]
                s = lax.dot_general(qm, kn[:, mi * dh:(mi + 1) * dh], _NT, preferred_element_type=F32) - bias
                update(mi, rs, jnp.where(vis_n, s, NEG), vn)
            o = acc_ref[0, rs] / l_ref[0, rs] - lam * (acc_ref[1, rs] / l_ref[1, rs])
            ms = jnp.mean(o * o, axis=-1, keepdims=True)
            o_ref[:, hs] = (o * lax.rsqrt(ms + EPS) * sw_ref[...] * (1.0 - lam_init)).astype(o_ref.dtype)


def _decode_attention(q, k_new, v_new, cache_k, cache_v, layer, lam_q, lam_k, subln_w, lam_init):
    bsz, tq, width = q.shape
    hw = width // ATT_HEADS
    past = cache_k.shape[2] // ATT_HEADS
    pblk = min(past, DECODE_PAST_BLOCK)
    rows = ATT_HEADS * tq
    assert past % pblk == 0 and tq & (tq - 1) == 0 and ATT_HEADS & (ATT_HEADS - 1) == 0
    srow = np.repeat(_alibi_slopes() * LOG2E, tq)[:, None] * np.ones((1, V7X_LANES))
    full = lambda shape: pl.BlockSpec(shape, lambda b, p: (0,) * len(shape))
    new = pl.BlockSpec((None, tq, width), lambda b, p: (b, 0, 0))
    cache = pl.BlockSpec((None, None, pblk * ATT_HEADS, hw), lambda b, p: (layer, b, p, 0))
    return pl.pallas_call(
        functools.partial(_decode_kernel, lam_init=lam_init, past=past),
        grid=(bsz, past // pblk),
        in_specs=[new, new, new, cache, cache, full((rows, V7X_LANES)), full(lam_q.shape), full(lam_k.shape),
                  full((1, hw))],
        out_specs=new,
        out_shape=jax.ShapeDtypeStruct((bsz, tq, width), BF16),
        scratch_shapes=[pltpu.VMEM((rows, pblk * ATT_HEADS), F32),
                        pltpu.VMEM((2, rows, 1), F32), pltpu.VMEM((2, rows, 1), F32),
                        pltpu.VMEM((2, rows, hw), F32)],
        compiler_params=_cparams(2),
        name="decode_diff_attn",
    )(q, k_new, v_new, cache_k, cache_v, jnp.asarray(srow, F32), lam_q.astype(F32), lam_k.astype(F32),
      subln_w.astype(F32).reshape(1, hw))


QKV_TN = 1024
SSD_IN_TN = 1024
SSD_CHUNK = 128


def _ssd_layer(h, bsz, norm_w, w_in, w_dt, conv0, h0, conv_w, conv_b, dt_bias, a_log, d_skip, gnorm_w, w_out,
               d_inner, conv_dim):
    m = h.shape[0]
    seq = m // bsz
    z, xbc, dt = _norm_mm(
        h, norm_w, w_in,
        [(d_inner // SSD_IN_TN, [(BF16, 1.0)]), (conv_dim // SSD_IN_TN, [(BF16, 1.0)])],
        tn=SSD_IN_TN, w_side=w_dt)
    g, conv_new, h_new = _ssd_scan(
        z.reshape(bsz, seq, d_inner), xbc.reshape(bsz, seq, conv_dim), dt.reshape(bsz, seq, V7X_LANES),
        conv0, h0, conv_w, conv_b, dt_bias, a_log, d_skip, gnorm_w, chunk=SSD_CHUNK)
    return _mm_res(g.reshape(m, d_inner), w_out, h), conv_new, h_new


def _qkv(h, norm_w, w_qkv, width):
    dh = width // (2 * ATT_HEADS)
    nt = width // QKV_TN
    return _norm_mm(h, norm_w, w_qkv, [(nt, [(BF16, dh ** -0.5 * LOG2E)]), (nt, [(F32, 1.0)]), (nt, [(F32, 1.0)])],
                    tn=QKV_TN)


def kernel(x_prompt, x_sample, cache_k, cache_v, state_ssm, state_conv, norm_mix_w, norm_mlp_w, final_norm_w, ssd_w_in, ssd_conv_w, ssd_conv_b, ssd_dt_bias, ssd_a_log, ssd_d, ssd_norm_w, ssd_w_out, att_w_qkv, att_lam_q, att_lam_k, att_subln_w, att_w_o, mlp_w_up, mlp_w_down):
    bp, sp, d_model = x_prompt.shape
    bs, ss, _ = x_sample.shape
    depth = norm_mix_w.shape[0]
    d_inner = ssd_w_out.shape[1]
    conv_dim = ssd_conv_w.shape[2]
    n_heads = ssd_d.shape[1]
    att_width = att_w_o.shape[1]
    hw = att_width // ATT_HEADS

    hp = x_prompt.reshape(bp * sp, d_model)
    hs = x_sample.reshape(bs * ss, d_model)
    cache_k = cache_k.reshape(cache_k.shape[0], bs, -1, hw)
    cache_v = cache_v.reshape(cache_v.shape[0], bs, -1, hw)

    k_p, v_p, ssm_p, conv_p, k_s, v_s, ssm_s, conv_s = [], [], [], [], [], [], [], []
    for i in range(depth):
        j = i // 2
        if i % 2 == 0:
            w_in = ssd_w_in[j][:, :d_inner + conv_dim].astype(BF16)
            w_dt = jnp.pad(ssd_w_in[j][:, d_inner + conv_dim:], ((0, 0), (0, V7X_LANES - n_heads))).astype(BF16)
            w_out = ssd_w_out[j].astype(BF16)
            args = (ssd_conv_w[j], ssd_conv_b[j], ssd_dt_bias[j], ssd_a_log[j], ssd_d[j], ssd_norm_w[j], w_out,
                    d_inner, conv_dim)
            hp, cp, sp_state = _ssd_layer(
                hp, bp, norm_mix_w[i], w_in, w_dt, jnp.zeros((bp, D_CONV - 1, conv_dim), F32),
                jnp.zeros((bp, n_heads, SSD_HEAD_DIM, D_STATE), F32), *args)
            hs, cs, ss_state = _ssd_layer(hs, bs, norm_mix_w[i], w_in, w_dt, state_conv[j], state_ssm[j], *args)
            conv_p.append(cp)
            ssm_p.append(sp_state)
            conv_s.append(cs)
            ssm_s.append(ss_state)
        else:
            w_qkv = att_w_qkv[j].astype(BF16)
            w_o = att_w_o[j].astype(BF16)
            lam_init = 0.8 - 0.6 * math.exp(-0.3 * i)
            qp, kp, vp = _qkv(hp, norm_mix_w[i], w_qkv, att_width)
            op = _flash_attention(qp.reshape(bp, sp, att_width), kp.reshape(bp, sp, att_width),
                                  vp.reshape(bp, sp, att_width), att_lam_q[j], att_lam_k[j], att_subln_w[j], lam_init)
            hp = _mm_res(op.reshape(bp * sp, att_width), w_o, hp)
            qs, ks, vs = _qkv(hs, norm_mix_w[i], w_qkv, att_width)
            os_ = _decode_attention(qs.reshape(bs, ss, att_width), ks.reshape(bs, ss, att_width),
                                    vs.reshape(bs, ss, att_width), cache_k, cache_v, j,
                                    att_lam_q[j], att_lam_k[j], att_subln_w[j], lam_init)
            hs = _mm_res(os_.reshape(bs * ss, att_width), w_o, hs)
            k_p.append(kp.reshape(bp, sp, ATT_HEADS, hw))
            v_p.append(vp.reshape(bp, sp, ATT_HEADS, hw))
            k_s.append(ks.reshape(bs, ss, ATT_HEADS, hw))
            v_s.append(vs.reshape(bs, ss, ATT_HEADS, hw))
        w_up = mlp_w_up[i].astype(BF16)
        w_down = mlp_w_down[i].astype(BF16)
        final_w = final_norm_w if i == depth - 1 else None
        hp = _mlp(hp, norm_mlp_w[i], w_up, w_down, final_w)
        hs = _mlp(hs, norm_mlp_w[i], w_up, w_down, final_w)

    y_prompt = hp.reshape(bp, sp, d_model)
    y_sample = hs.reshape(bs, ss, d_model)
    return (y_prompt, y_sample, jnp.stack(k_p), jnp.stack(v_p), jnp.stack(ssm_p), jnp.stack(conv_p),
            jnp.stack(k_s), jnp.stack(v_s), jnp.stack(ssm_s), jnp.stack(conv_s))
```

```python
import functools
import math

import numpy as np
import jax
import jax.numpy as jnp
from jax import lax
from jax.experimental import pallas as pl
from jax.experimental.pallas import tpu as pltpu

F32 = jnp.float32
BF16 = jnp.bfloat16

EPS = 1e-5
CHUNK = 64
CHUNK_SHIFT = 6
D_CONV = 4
SSD_HEAD_DIM = 64
SSD_GROUPS = 8
D_STATE = 128
ATT_HEADS = 8
NEG = -1e30
LOG2E = math.log2(math.e)

V7X_LANES = 128
V7X_SUBLANES = 8
V7X_BF16_ROWS = 16
V7X_VMEM_LIMIT_BYTES = 56 * 1024 * 1024

_NT = (((1,), (1,)), ((), ()))


def _cparams(n_axes):
    return pltpu.CompilerParams(dimension_semantics=("arbitrary",) * n_axes,
                                vmem_limit_bytes=V7X_VMEM_LIMIT_BYTES)


def _silu(u):
    half = 0.5 * u
    return half * jnp.tanh(half) + half


def _softplus(u):
    return jnp.maximum(u, 0.0) + jnp.log(1.0 + jnp.exp(-jnp.abs(u)))


def _split3(x):
    hi = x.astype(BF16)
    r = x - hi.astype(F32)
    mid = r.astype(BF16)
    lo = (r - mid.astype(F32)).astype(BF16)
    return hi, mid, lo


def _dot01(sel, x):
    hi, mid, lo = _split3(x)
    return (jnp.dot(sel, hi, preferred_element_type=F32) + jnp.dot(sel, mid, preferred_element_type=F32)
            + jnp.dot(sel, lo, preferred_element_type=F32))


def _normalise_rows(x_ref, nw_ref, xn_ref, *, static_rows=False):
    nw = nw_ref[...]

    def group(r0):
        x = x_ref[pl.ds(r0, NORM_ROWS), :]
        ms = jnp.mean(x * x, axis=-1, keepdims=True)
        xn_ref[pl.ds(r0, NORM_ROWS), :] = (x * lax.rsqrt(ms + EPS) * nw).astype(xn_ref.dtype)

    if static_rows:
        for c in range(x_ref.shape[0] // NORM_ROWS):
            group(c * NORM_ROWS)
        return

    def body(c, carry):
        group(pl.multiple_of(c * NORM_ROWS, NORM_ROWS))
        return carry

    lax.fori_loop(0, x_ref.shape[0] // NORM_ROWS, body, 0, unroll=NORM_UNROLL)


NORM_ROWS = 16
NORM_UNROLL = 8


def _norm_mm_kernel(*refs, seg_lo, seg_hi, seg_outs, has_side):
    n_in = 4 if has_side else 3
    x_ref, nw_ref, w_ref = refs[:3]
    out_refs, xn_ref = refs[n_in:-1], refs[-1]
    j = pl.program_id(1)

    @pl.when(j == 0)
    def _normalise():
        _normalise_rows(x_ref, nw_ref, xn_ref)
        if has_side:
            out_refs[-1][...] = jnp.dot(xn_ref[...], refs[3][...], preferred_element_type=F32)

    acc = jnp.dot(xn_ref[...], w_ref[...], preferred_element_type=F32)

    k = 0
    for s, outs in enumerate(seg_outs):
        seg_refs = out_refs[k:k + len(outs)]
        k += len(outs)

        def write(seg_refs=seg_refs, outs=outs):
            for o_ref, scale in zip(seg_refs, outs):
                o_ref[...] = (acc if scale == 1.0 else acc * scale).astype(o_ref.dtype)

        if len(seg_outs) == 1:
            write()
        else:
            pl.when((j >= seg_lo[s]) & (j < seg_hi[s]))(write)


def _norm_mm(x, nw, w, segs, *, tn, w_side=None):
    m, kdim = x.shape
    n = w.shape[1]
    tm = min(m, 1024)
    assert m % tm == 0 and n % tn == 0 and tm % NORM_ROWS == 0
    lo, seg_lo, seg_hi = 0, [], []
    for nt, _ in segs:
        seg_lo.append(lo)
        lo += nt
        seg_hi.append(lo)
    assert lo == n // tn
    out_shape, out_specs = [], []
    for (nt, outs), s_lo in zip(segs, seg_lo):
        for dtype, _ in outs:
            out_shape.append(jax.ShapeDtypeStruct((m, nt * tn), dtype))
            out_specs.append(pl.BlockSpec(
                (tm, tn), lambda i, j, s_lo=s_lo, nt=nt: (i, jnp.clip(j - s_lo, 0, nt - 1))))
    in_specs = [pl.BlockSpec((tm, kdim), lambda i, j: (i, 0)),
                pl.BlockSpec((1, kdim), lambda i, j: (0, 0)),
                pl.BlockSpec((kdim, tn), lambda i, j: (0, j))]
    args = [x, nw.reshape(1, kdim), w]
    if w_side is not None:
        in_specs.append(pl.BlockSpec(w_side.shape, lambda i, j: (0, 0)))
        args.append(w_side)
        out_shape.append(jax.ShapeDtypeStruct((m, w_side.shape[1]), F32))
        out_specs.append(pl.BlockSpec((tm, w_side.shape[1]), lambda i, j: (i, 0)))
    kern = functools.partial(
        _norm_mm_kernel, seg_lo=tuple(seg_lo), seg_hi=tuple(seg_hi),
        seg_outs=tuple(tuple(float(sc) for _, sc in s[1]) for s in segs), has_side=w_side is not None)
    return pl.pallas_call(
        kern,
        grid=(m // tm, n // tn),
        in_specs=in_specs,
        out_specs=out_specs,
        out_shape=out_shape,
        scratch_shapes=[pltpu.VMEM((tm, kdim), BF16)],
        compiler_params=_cparams(2),
        name="norm_mm",
    )(*args)


def _mm_res_kernel(a_ref, w_ref, r_ref, o_ref):
    o_ref[...] = r_ref[...] + jnp.dot(a_ref[...], w_ref[...], preferred_element_type=F32)


MM_RES_LHS_BLOCK_BYTES = 2 * 1024 * 1024


def _mm_res(a, w, res):
    m, kdim = a.shape
    n = w.shape[1]
    tm = min(m, MM_RES_LHS_BLOCK_BYTES // (kdim * a.dtype.itemsize))
    assert m % tm == 0
    return pl.pallas_call(
        _mm_res_kernel,
        grid=(m // tm,),
        in_specs=[pl.BlockSpec((tm, kdim), lambda i: (i, 0)),
                  pl.BlockSpec((kdim, n), lambda i: (0, 0), pipeline_mode=pl.Buffered(1)),
                  pl.BlockSpec((tm, n), lambda i: (i, 0))],
        out_specs=pl.BlockSpec((tm, n), lambda i: (i, 0)),
        out_shape=jax.ShapeDtypeStruct((m, n), F32),
        compiler_params=_cparams(1),
        name="mm_res",
    )(a, w, res)


def _mlp_kernel(x_ref, nw_ref, wu_ref, wd_ref, *rest, final_norm):
    fw_ref = rest[0] if final_norm else None
    o_ref, xn_ref = rest[-2], rest[-1]
    j = pl.program_id(1)

    @pl.when(j == 0)
    def _start():
        _normalise_rows(x_ref, nw_ref, xn_ref)
        o_ref[...] = x_ref[...]

    h = jnp.dot(xn_ref[...], wu_ref[...], preferred_element_type=F32)
    h = jnp.square(jnp.maximum(h, 0.0)).astype(BF16)
    o_ref[...] += jnp.dot(h, wd_ref[...], preferred_element_type=F32)

    if final_norm:
        @pl.when(j == pl.num_programs(1) - 1)
        def _final_norm():
            _normalise_rows(o_ref, fw_ref, o_ref, static_rows=True)


def _mlp(x, nw, w_up, w_down, final_w=None, *, tf=512):
    m, d = x.shape
    d_ff = w_up.shape[1]
    tm = min(m, 1024)
    assert m % tm == 0 and d_ff % tf == 0
    in_specs = [pl.BlockSpec((tm, d), lambda i, j: (i, 0)),
                pl.BlockSpec((1, d), lambda i, j: (0, 0)),
                pl.BlockSpec((d, tf), lambda i, j: (0, j)),
                pl.BlockSpec((tf, d), lambda i, j: (j, 0))]
    args = [x, nw.reshape(1, d), w_up, w_down]
    if final_w is not None:
        in_specs.append(pl.BlockSpec((1, d), lambda i, j: (0, 0)))
        args.append(final_w.reshape(1, d))
    return pl.pallas_call(
        functools.partial(_mlp_kernel, final_norm=final_w is not None),
        grid=(m // tm, d_ff // tf),
        in_specs=in_specs,
        out_specs=pl.BlockSpec((tm, d), lambda i, j: (i, 0)),
        out_shape=jax.ShapeDtypeStruct((m, d), F32),
        scratch_shapes=[pltpu.VMEM((tm, d), BF16)],
        compiler_params=_cparams(2),
        name="mlp",
    )(*args)


CONV_SLAB = 512
CONV_UNROLL = 4
HALO = V7X_BF16_ROWS


def _ssd_kernel(z_ref, xbc_ref, dt_ref, conv0_ref, h0_ref, cw_ref, cb_ref, dtb_ref, alog_ref,
                dsk_ref, nw_ref, e_ref, shift_ref, g_ref, convo_ref, ho_ref, xp_ref, ht_ref, xs_ref, bc_ref):
    t = pl.program_id(1)
    nt = pl.num_programs(1)
    L = z_ref.shape[0]
    d_inner = z_ref.shape[1]
    n_bc = bc_ref.shape[1]
    gw = d_inner // SSD_GROUPS
    hpg = gw // SSD_HEAD_DIM

    eye_r = lax.broadcasted_iota(jnp.int32, (V7X_LANES, V7X_LANES), 0)
    eye_c = lax.broadcasted_iota(jnp.int32, (V7X_LANES, V7X_LANES), 1)
    eye_b = (eye_r == eye_c).astype(F32).astype(BF16)

    @pl.when(t == 0)
    def _init():
        xp_ref[0:HALO, :] = conv0_ref[...]
        for g in range(SSD_GROUPS):
            ht_ref[g] = jnp.transpose(h0_ref[g])

    xp_ref[HALO:HALO + L, :] = xbc_ref[...]

    def conv_slab(c0):
        taps = jnp.dot(shift_ref[...], xp_ref[:, pl.ds(c0, CONV_SLAB)], preferred_element_type=F32)
        u = cb_ref[:, pl.ds(c0, CONV_SLAB)]
        for k in range(D_CONV):
            u = u + cw_ref[pl.ds(k, 1), pl.ds(c0, CONV_SLAB)] * taps[k * L:(k + 1) * L]
        return _silu(u)

    def x_body(c, carry):
        c0 = pl.multiple_of(c * CONV_SLAB, CONV_SLAB)
        xs_ref[:, pl.ds(c0, CONV_SLAB)] = conv_slab(c0)
        return carry

    lax.fori_loop(0, d_inner // CONV_SLAB, x_body, 0, unroll=CONV_UNROLL)

    def bc_body(c, carry):
        c0 = pl.multiple_of(c * CONV_SLAB, CONV_SLAB)
        bc_ref[:, pl.ds(c0, CONV_SLAB)] = conv_slab(d_inner + c0).astype(BF16)
        return carry

    lax.fori_loop(0, n_bc // CONV_SLAB, bc_body, 0, unroll=CONV_UNROLL)

    dt = _softplus(dt_ref[...] + dtb_ref[...])
    a = dt * (-jnp.exp(alog_ref[...]))
    ri = lax.broadcasted_iota(jnp.int32, (L, L), 0)
    ci = lax.broadcasted_iota(jnp.int32, (L, L), 1)
    tri = ri >= ci
    acum = _dot01(tri.astype(F32).astype(BF16), a)
    a_hi, a_mid, a_lo = _split3(acum)
    acum_t = (lax.dot_general(eye_b, a_hi, _NT, preferred_element_type=F32)
              + lax.dot_general(eye_b, a_mid, _NT, preferred_element_type=F32)
              + lax.dot_general(eye_b, a_lo, _NT, preferred_element_type=F32))
    last = acum[L - 1:L, :]
    per_head = jnp.concatenate(
        [dt.astype(BF16), jnp.exp(last - acum).astype(BF16), jnp.exp(acum).astype(BF16)], axis=0)
    cd8 =jnp.broadcast_to(jnp.exp(last), (V7X_SUBLANES, V7X_LANES))
    c_hi, c_mid, c_lo = _split3(cd8)
    lane = lax.broadcasted_iota(jnp.int32, (L, V7X_LANES), 1)
    lane_lo = lane < SSD_HEAD_DIM

    for g in range(SSD_GROUPS):
        cs = slice(g * gw, (g + 1) * gw)
        e_g = e_ref[:, cs]
        ex = jnp.dot(per_head, e_g, preferred_element_type=F32)
        dtx, tlx, eax = ex[0:L], ex[L:2 * L], ex[2 * L:3 * L]
        cdx = (jnp.dot(c_hi, e_g, preferred_element_type=F32) + jnp.dot(c_mid, e_g, preferred_element_type=F32)
               + jnp.dot(c_lo, e_g, preferred_element_type=F32))[0:1, :]
        xs_g = xs_ref[:, cs]
        xdt = xs_g * dtx
        b_g = bc_ref[:, g * D_STATE:(g + 1) * D_STATE]
        c_g = bc_ref[:, n_bc // 2 + g * D_STATE:n_bc // 2 + (g + 1) * D_STATE]
        cb = lax.dot_general(c_g, b_g, _NT, preferred_element_type=F32)
        ht = ht_ref[g]
        y_state = jnp.dot(c_g, ht.astype(BF16), preferred_element_type=F32) * eax
        b_gt = lax.dot_general(eye_b, b_g, _NT, preferred_element_type=F32).astype(BF16)
        ht_ref[g] = ht * cdx + jnp.dot(b_gt, (xdt * tlx).astype(BF16), preferred_element_type=F32)

        pairs = []
        for p in range(gw // V7X_LANES):
            ls = slice(p * V7X_LANES, (p + 1) * V7X_LANES)
            xdt_p = xdt[:, ls]
            y_p = y_state[:, ls] + xs_g[:, ls] * dsk_ref[:, g * gw + p * V7X_LANES:g * gw + (p + 1) * V7X_LANES]
            for q in range(V7X_LANES // SSD_HEAD_DIM):
                h = g * hpg + p * (V7X_LANES // SSD_HEAD_DIM) + q
                seg = acum[:, h:h + 1] - acum_t[h:h + 1, :]
                m = (jnp.exp(jnp.where(tri, seg, NEG)) * cb).astype(BF16)
                rhs = jnp.where(lane_lo if q == 0 else jnp.logical_not(lane_lo), xdt_p, 0.0).astype(BF16)
                y_p = y_p + jnp.dot(m, rhs, preferred_element_type=F32)
            pairs.append(y_p)
        y_g = jnp.concatenate(pairs, axis=1)
        gate = y_g * _silu(z_ref[:, cs].astype(F32))
        ms = jnp.mean(gate * gate, axis=-1, keepdims=True)
        g_ref[:, cs] = (gate * lax.rsqrt(ms + EPS) * nw_ref[:, cs]).astype(g_ref.dtype)

    xp_ref[0:HALO, :] = xp_ref[L:L + HALO, :]

    @pl.when(t == nt - 1)
    def _finish():
        convo_ref[...] = xp_ref[0:HALO, :].astype(F32)[HALO - (D_CONV - 1):HALO, :]
        for g in range(SSD_GROUPS):
            ho_ref[g] = jnp.transpose(ht_ref[g])


def _ssd_scan(z, xbc, dt, conv0, h0, conv_w, conv_b, dt_bias, a_log, d_skip, norm_w, *, chunk):
    bsz, seq, d_inner = z.shape
    conv_dim = xbc.shape[2]
    n_heads = d_inner // SSD_HEAD_DIM
    gw = d_inner // SSD_GROUPS
    L = min(chunk, seq)
    assert seq % L == 0 and n_heads <= V7X_LANES and L % HALO == 0
    pad = V7X_LANES - n_heads
    expand = np.zeros((V7X_LANES, d_inner), np.float32)
    expand[np.arange(d_inner) // SSD_HEAD_DIM, np.arange(d_inner)] = 1.0
    shift = np.zeros((D_CONV * L, L + HALO), np.float32)
    for k in range(D_CONV):
        shift[k * L + np.arange(L), HALO - (D_CONV - 1) + k + np.arange(L)] = 1.0
    conv0_p = jnp.pad(conv0.astype(BF16), ((0, 0), (HALO - (D_CONV - 1), 0), (0, 0)))
    row = lambda v: v.astype(F32).reshape(1, -1)
    full = lambda shape: pl.BlockSpec(shape, lambda b, t: (0,) * len(shape))
    g_out, conv_out, h_out = pl.pallas_call(
        _ssd_kernel,
        grid=(bsz, seq // L),
        in_specs=[pl.BlockSpec((None, L, d_inner), lambda b, t: (b, t, 0)),
                  pl.BlockSpec((None, L, conv_dim), lambda b, t: (b, t, 0)),
                  pl.BlockSpec((None, L, V7X_LANES), lambda b, t: (b, t, 0)),
                  pl.BlockSpec((None, HALO, conv_dim), lambda b, t: (b, 0, 0)),
                  pl.BlockSpec((None, SSD_GROUPS, gw, D_STATE), lambda b, t: (b, 0, 0, 0)),
                  full((D_CONV, conv_dim)), full((1, conv_dim)), full((1, V7X_LANES)), full((1, V7X_LANES)),
                  full((1, d_inner)), full((1, d_inner)), full((V7X_LANES, d_inner)),
                  full((D_CONV * L, L + HALO))],
        out_specs=[pl.BlockSpec((None, L, d_inner), lambda b, t: (b, t, 0)),
                   pl.BlockSpec((None, D_CONV - 1, conv_dim), lambda b, t: (b, 0, 0)),
                   pl.BlockSpec((None, SSD_GROUPS, gw, D_STATE), lambda b, t: (b, 0, 0, 0))],
        out_shape=[jax.ShapeDtypeStruct((bsz, seq, d_inner), BF16),
                   jax.ShapeDtypeStruct((bsz, D_CONV - 1, conv_dim), F32),
                   jax.ShapeDtypeStruct((bsz, SSD_GROUPS, gw, D_STATE), F32)],
        scratch_shapes=[pltpu.VMEM((L + HALO, conv_dim), BF16),
                        pltpu.VMEM((SSD_GROUPS, D_STATE, gw), F32),
                        pltpu.VMEM((L, d_inner), F32),
                        pltpu.VMEM((L, conv_dim - d_inner), BF16)],
        compiler_params=_cparams(2),
        name="ssd_scan",
    )(z, xbc, dt, conv0_p, h0.astype(F32).reshape(bsz, SSD_GROUPS, gw, D_STATE),
      conv_w.astype(F32), row(conv_b), jnp.pad(row(dt_bias), ((0, 0), (0, pad))),
      jnp.pad(row(a_log), ((0, 0), (0, pad))), row(jnp.repeat(d_skip, SSD_HEAD_DIM)), row(norm_w),
      jnp.asarray(expand, BF16), jnp.asarray(shift, BF16))
    return g_out, conv_out, h_out.reshape(bsz, n_heads, SSD_HEAD_DIM, D_STATE)


def _alibi_slopes():
    return 2.0 ** (-8.0 * np.arange(1, ATT_HEADS + 1, dtype=np.float64) / ATT_HEADS)


def _lam(lq_ref, lk_ref, lam_init):
    e = jnp.exp(jnp.sum(lq_ref[...] * lk_ref[...], axis=-1, keepdims=True))
    return e[0:1, :] - e[1:2, :] + lam_init


FIRST_SLOT = 2


def _flash_kernel(slope_ref, q_ref, k_ref, v_ref, lq_ref, lk_ref, swt_ref, o_ref,
                  kb_ref, vt_ref, s_ref, mx_ref, m_ref, l_ref, acc_ref, *, lam_init):
    h = pl.program_id(1)
    qi = pl.program_id(2)
    tq = o_ref.shape[0]
    tk = tq
    seq = k_ref.shape[0]
    dh = q_ref.shape[1] // 2
    slope = slope_ref[h]

    @pl.when(qi == 0)
    def _stage_kv():
        def body(c, carry):
            r0 = pl.multiple_of(c * tk, tk)
            kb_ref[pl.ds(r0, tk), :] = k_ref[pl.ds(r0, tk), :].astype(BF16)
            vt_ref[:, pl.ds(r0, tk)] = jnp.transpose(v_ref[pl.ds(r0, tk), :]).astype(BF16)
            return carry
        lax.fori_loop(0, seq // tk, body, 0)

    m_ref[...] = jnp.full(m_ref.shape, NEG, F32)
    l_ref[...] = jnp.zeros(l_ref.shape, F32)
    acc_ref[...] = jnp.zeros(acc_ref.shape, F32)

    nq = pl.num_programs(2)
    kidx = lax.broadcasted_iota(jnp.int32, (tk, tq), 0)
    qidx = lax.broadcasted_iota(jnp.int32, (tk, tq), 1)
    rel = (qidx - kidx).astype(F32)

    def qk(qb, kj, mi):
        q0 = pl.multiple_of(qb * tq, tq)
        r0 = pl.multiple_of(kj * tk, tk)
        return lax.dot_general(kb_ref[pl.ds(r0, tk), mi * dh:(mi + 1) * dh],
                               q_ref[pl.ds(q0, tq), mi * dh:(mi + 1) * dh], _NT, preferred_element_type=F32)

    def scores(qb, kj, slot):
        bias = slope * rel
        for mi in range(2):
            s = qk(qb, kj, mi) - bias
            s_ref[slot, mi] = s
            mx_ref[slot, mi] = jnp.max(s, axis=0, keepdims=True)

    def scores_diag(slot):
        visible = (kidx >> CHUNK_SHIFT) <= (qidx >> CHUNK_SHIFT)
        bias = slope * jnp.abs(rel)
        for mi in range(2):
            s = jnp.where(visible, qk(qi, qi, mi) - bias, NEG)
            s_ref[slot, mi] = s
            mx_ref[slot, mi] = jnp.max(s, axis=0, keepdims=True)

    def accumulate(kj, slot, off):
        vt_blk = vt_ref[:, pl.ds(pl.multiple_of(kj * tk, tk), tk)]
        for mi in range(2):
            m_prev = m_ref[mi]
            m_new = jnp.maximum(m_prev, mx_ref[slot, mi] - off)
            p = jnp.exp2(s_ref[slot, mi] - (m_new + off))
            alpha = jnp.exp2(m_prev - m_new)
            l_ref[mi] = alpha * l_ref[mi] + jnp.sum(p, axis=0, keepdims=True)
            acc_ref[mi] = alpha * acc_ref[mi] + jnp.dot(vt_blk, p.astype(BF16), preferred_element_type=F32)
            m_ref[mi] = m_new

    def block_offset(kj):
        return slope * ((qi - kj) * tq).astype(F32)

    def slot_of(kj):
        return jnp.where(kj == 0, FIRST_SLOT, kj & 1)

    def body(kj, carry):
        accumulate(kj, slot_of(kj), block_offset(kj))
        scores(qi, kj + 1, (kj + 1) & 1)
        return carry

    lax.fori_loop(0, qi - 1, body, 0)

    @pl.when(qi > 0)
    def _last_earlier_block():
        accumulate(qi - 1, slot_of(qi - 1), block_offset(qi - 1))
        scores_diag(qi & 1)

    @pl.when(qi == 0)
    def _only_diag():
        scores_diag(0)

    accumulate(qi, qi & 1, 0.0)
    scores(jnp.minimum(qi + 1, nq - 1), 0, FIRST_SLOT)

    o = acc_ref[0] / l_ref[0] - _lam(lq_ref, lk_ref, lam_init) * (acc_ref[1] / l_ref[1])
    ms = jnp.mean(o * o, axis=0, keepdims=True)
    o = o * lax.rsqrt(ms + EPS) * swt_ref[...] * (1.0 - lam_init)
    o_ref[...] = jnp.transpose(o).astype(o_ref.dtype)


def _flash_attention(q, k, v, lam_q, lam_k, subln_w, lam_init, *, tq=512):
    bsz, seq, width = q.shape
    hw = width // ATT_HEADS
    tq = min(tq, seq)
    assert seq % tq == 0 and tq % CHUNK == 0
    smem = pl.BlockSpec(memory_space=pltpu.SMEM)
    full = lambda shape: pl.BlockSpec(shape, lambda b, h, i: (0,) * len(shape))
    swt = jnp.broadcast_to(subln_w.astype(F32)[:, None], (hw, tq))
    return pl.pallas_call(
        functools.partial(_flash_kernel, lam_init=lam_init),
        grid=(bsz, ATT_HEADS, seq // tq),
        in_specs=[smem,
                  pl.BlockSpec((None, seq, hw), lambda b, h, i: (b, 0, h)),
                  pl.BlockSpec((None, seq, hw), lambda b, h, i: (b, 0, h)),
                  pl.BlockSpec((None, seq, hw), lambda b, h, i: (b, 0, h)),
                  full(lam_q.shape), full(lam_k.shape), full((hw, tq))],
        out_specs=pl.BlockSpec((None, tq, hw), lambda b, h, i: (b, i, h)),
        out_shape=jax.ShapeDtypeStruct((bsz, seq, width), BF16),
        scratch_shapes=[pltpu.VMEM((seq, hw), BF16), pltpu.VMEM((hw, seq), BF16),
                        pltpu.VMEM((FIRST_SLOT + 1, 2, tq, tq), F32), pltpu.VMEM((FIRST_SLOT + 1, 2, 1, tq), F32),
                        pltpu.VMEM((2, 1, tq), F32), pltpu.VMEM((2, 1, tq), F32),
                        pltpu.VMEM((2, hw, tq), F32)],
        compiler_params=_cparams(3),
        name="flash_diff_attn",
    )(jnp.asarray(_alibi_slopes() * LOG2E, F32), q, k, v, lam_q.astype(F32), lam_k.astype(F32), swt)


DECODE_PAST_BLOCK = 512


def _decode_kernel(q_ref, kn_ref, vn_ref, kc_ref, vc_ref, srow_ref, lq_ref, lk_ref, sw_ref, o_ref,
                   mb_ref, m_ref, l_ref, acc_ref, *, lam_init, past):
    pi = pl.program_id(1)
    npb = pl.num_programs(1)
    tq = q_ref.shape[0]
    rows = ATT_HEADS * tq
    cols = kc_ref.shape[0]
    pblk = cols // ATT_HEADS
    hw = sw_ref.shape[1]
    dh = hw // 2
    t_shift = tq.bit_length() - 1
    h_shift = ATT_HEADS.bit_length() - 1
    srow = srow_ref[:, 0:1]

    @pl.when(pi == 0)
    def _init():
        m_ref[...] = jnp.full(m_ref.shape, NEG, F32)
        l_ref[...] = jnp.zeros(l_ref.shape, F32)
        acc_ref[...] = jnp.zeros(acc_ref.shape, F32)
        r = lax.broadcasted_iota(jnp.int32, (rows, cols), 0)
        c = lax.broadcasted_iota(jnp.int32, (rows, cols), 1)
        same_head = (c & (ATT_HEADS - 1)) == (r >> t_shift)
        rel = ((r & (tq - 1)) - (c >> h_shift)).astype(F32)
        mb_ref[...] = jnp.where(same_head, -(srow * rel), NEG)

    def update(mi, rs, s, vblk):
        m_prev = m_ref[mi, rs]
        m_new = jnp.maximum(m_prev, jnp.max(s, axis=-1, keepdims=True))
        p = jnp.exp2(s - m_new)
        alpha = jnp.exp2(m_prev - m_new)
        l_ref[mi, rs] = alpha * l_ref[mi, rs] + jnp.sum(p, axis=-1, keepdims=True)
        acc_ref[mi, rs] = alpha * acc_ref[mi, rs] + jnp.dot(p.astype(BF16), vblk, preferred_element_type=F32)
        m_ref[mi, rs] = m_new

    off = srow * (past - pi * pblk).astype(F32)
    kc = kc_ref[...].astype(BF16)
    vc = vc_ref[...].astype(BF16)
    for mi in range(2):
        qm = jnp.concatenate([q_ref[:, h * hw + mi * dh:h * hw + (mi + 1) * dh] for h in range(ATT_HEADS)], axis=0)
        s = lax.dot_general(qm, kc[:, mi * dh:(mi + 1) * dh], _NT, preferred_element_type=F32)
        update(mi, slice(None), s + mb_ref[...] - off, vc)

    @pl.when(pi == npb - 1)
    def _finish():
        slopes = _alibi_slopes() * LOG2E
        ri = lax.broadcasted_iota(jnp.int32, (tq, tq), 0)
        ci = lax.broadcasted_iota(jnp.int32, (tq, tq), 1)
        dist_n = jnp.abs(ri - ci).astype(F32)
        vis_n = ((past + ci) >> CHUNK_SHIFT) <= ((past + ri) >> CHUNK_SHIFT)
        lam = _lam(lq_ref, lk_ref, lam_init)
        for h in range(ATT_HEADS):
            hs = slice(h * hw, (h + 1) * hw)
            rs = slice(h * tq, (h + 1) * tq)
            kn = kn_ref[:, hs].astype(BF16)
            vn = vn_ref[:, hs].astype(BF16)
            bias = float(slopes[h]) * dist_n
            for mi in range(2):
                qm = q_ref[:, h * hw + mi * dh:h * hw + (mi + 1) * dh]
                s = lax.dot_general(qm, kn[:, mi * dh:(mi + 1) * dh], _NT, preferred_element_type=F32) - bias
                update(mi, rs, jnp.where(vis_n, s, NEG), vn)
            o = acc_ref[0, rs] / l_ref[0, rs] - lam * (acc_ref[1, rs] / l_ref[1, rs])
            ms = jnp.mean(o * o, axis=-1, keepdims=True)
            o_ref[:, hs] = (o * lax.rsqrt(ms + EPS) * sw_ref[...] * (1.0 - lam_init)).astype(o_ref.dtype)


def _decode_attention(q, k_new, v_new, cache_k, cache_v, layer, lam_q, lam_k, subln_w, lam_init):
    bsz, tq, width = q.shape
    hw = width // ATT_HEADS
    past = cache_k.shape[2] // ATT_HEADS
    pblk = min(past, DECODE_PAST_BLOCK)
    rows = ATT_HEADS * tq
    assert past % pblk == 0 and tq & (tq - 1) == 0 and ATT_HEADS & (ATT_HEADS - 1) == 0
    srow = np.repeat(_alibi_slopes() * LOG2E, tq)[:, None] * np.ones((1, V7X_LANES))
    full = lambda shape: pl.BlockSpec(shape, lambda b, p: (0,) * len(shape))
    new = pl.BlockSpec((None, tq, width), lambda b, p: (b, 0, 0))
    cache = pl.BlockSpec((None, None, pblk * ATT_HEADS, hw), lambda b, p: (layer, b, p, 0))
    return pl.pallas_call(
        functools.partial(_decode_kernel, lam_init=lam_init, past=past),
        grid=(bsz, past // pblk),
        in_specs=[new, new, new, cache, cache, full((rows, V7X_LANES)), full(lam_q.shape), full(lam_k.shape),
                  full((1, hw))],
        out_specs=new,
        out_shape=jax.ShapeDtypeStruct((bsz, tq, width), BF16),
        scratch_shapes=[pltpu.VMEM((rows, pblk * ATT_HEADS), F32),
                        pltpu.VMEM((2, rows, 1), F32), pltpu.VMEM((2, rows, 1), F32),
                        pltpu.VMEM((2, rows, hw), F32)],
        compiler_params=_cparams(2),
        name="decode_diff_attn",
    )(q, k_new, v_new, cache_k, cache_v, jnp.asarray(srow, F32), lam_q.astype(F32), lam_k.astype(F32),
      subln_w.astype(F32).reshape(1, hw))


QKV_TN = 1024
SSD_IN_TN = 1024
SSD_CHUNK = 128


def _ssd_layer(h, bsz, norm_w, w_in, w_dt, conv0, h0, conv_w, conv_b, dt_bias, a_log, d_skip, gnorm_w, w_out,
               d_inner, conv_dim):
    m = h.shape[0]
    seq = m // bsz
    z, xbc, dt = _norm_mm(
        h, norm_w, w_in,
        [(d_inner // SSD_IN_TN, [(BF16, 1.0)]), (conv_dim // SSD_IN_TN, [(BF16, 1.0)])],
        tn=SSD_IN_TN, w_side=w_dt)
    g, conv_new, h_new = _ssd_scan(
        z.reshape(bsz, seq, d_inner), xbc.reshape(bsz, seq, conv_dim), dt.reshape(bsz, seq, V7X_LANES),
        conv0, h0, conv_w, conv_b, dt_bias, a_log, d_skip, gnorm_w, chunk=SSD_CHUNK)
    return _mm_res(g.reshape(m, d_inner), w_out, h), conv_new, h_new


def _qkv(h, norm_w, w_qkv, width):
    dh = width // (2 * ATT_HEADS)
    nt = width // QKV_TN
    return _norm_mm(h, norm_w, w_qkv, [(nt, [(BF16, dh ** -0.5 * LOG2E)]), (nt, [(F32, 1.0)]), (nt, [(F32, 1.0)])],
                    tn=QKV_TN)


def kernel(x_prompt, x_sample, cache_k, cache_v, state_ssm, state_conv, norm_mix_w, norm_mlp_w, final_norm_w, ssd_w_in, ssd_conv_w, ssd_conv_b, ssd_dt_bias, ssd_a_log, ssd_d, ssd_norm_w, ssd_w_out, att_w_qkv, att_lam_q, att_lam_k, att_subln_w, att_w_o, mlp_w_up, mlp_w_down):
    bp, sp, d_model = x_prompt.shape
    bs, ss, _ = x_sample.shape
    depth = norm_mix_w.shape[0]
    d_inner = ssd_w_out.shape[1]
    conv_dim = ssd_conv_w.shape[2]
    n_heads = ssd_d.shape[1]
    att_width = att_w_o.shape[1]
    hw = att_width // ATT_HEADS

    hp = x_prompt.reshape(bp * sp, d_model)
    hs = x_sample.reshape(bs * ss, d_model)
    cache_k = cache_k.reshape(cache_k.shape[0], bs, -1, hw)
    cache_v = cache_v.reshape(cache_v.shape[0], bs, -1, hw)

    k_p, v_p, ssm_p, conv_p, k_s, v_s, ssm_s, conv_s = [], [], [], [], [], [], [], []
    for i in range(depth):
        j = i // 2
        if i % 2 == 0:
            w_in = ssd_w_in[j][:, :d_inner + conv_dim].astype(BF16)
            w_dt = jnp.pad(ssd_w_in[j][:, d_inner + conv_dim:], ((0, 0), (0, V7X_LANES - n_heads))).astype(BF16)
            w_out = ssd_w_out[j].astype(BF16)
            args = (ssd_conv_w[j], ssd_conv_b[j], ssd_dt_bias[j], ssd_a_log[j], ssd_d[j], ssd_norm_w[j], w_out,
                    d_inner, conv_dim)
            hp, cp, sp_state = _ssd_layer(
                hp, bp, norm_mix_w[i], w_in, w_dt, jnp.zeros((bp, D_CONV - 1, conv_dim), F32),
                jnp.zeros((bp, n_heads, SSD_HEAD_DIM, D_STATE), F32), *args)
            hs, cs, ss_state = _ssd_layer(hs, bs, norm_mix_w[i], w_in, w_dt, state_conv[j], state_ssm[j], *args)
            conv_p.append(cp)
            ssm_p.append(sp_state)
            conv_s.append(cs)
            ssm_s.append(ss_state)
        else:
            w_qkv = att_w_qkv[j].astype(BF16)
            w_o = att_w_o[j].astype(BF16)
            lam_init = 0.8 - 0.6 * math.exp(-0.3 * i)
            qp, kp, vp = _qkv(hp, norm_mix_w[i], w_qkv, att_width)
            op = _flash_attention(qp.reshape(bp, sp, att_width), kp.reshape(bp, sp, att_width),
                                  vp.reshape(bp, sp, att_width), att_lam_q[j], att_lam_k[j], att_subln_w[j], lam_init)
            hp = _mm_res(op.reshape(bp * sp, att_width), w_o, hp)
            qs, ks, vs = _qkv(hs, norm_mix_w[i], w_qkv, att_width)
            os_ = _decode_attention(qs.reshape(bs, ss, att_width), ks.reshape(bs, ss, att_width),
                                    vs.reshape(bs, ss, att_width), cache_k, cache_v, j,
                                    att_lam_q[j], att_lam_k[j], att_subln_w[j], lam_init)
            hs = _mm_res(os_.reshape(bs * ss, att_width), w_o, hs)
            k_p.append(kp.reshape(bp, sp, ATT_HEADS, hw))
            v_p.append(vp.reshape(bp, sp, ATT_HEADS, hw))
            k_s.append(ks.reshape(bs, ss, ATT_HEADS, hw))
            v_s.append(vs.reshape(bs, ss, ATT_HEADS, hw))
        w_up = mlp_w_up[i].astype(BF16)
        w_down = mlp_w_down[i].astype(BF16)
        final_w = final_norm_w if i == depth - 1 else None
        hp = _mlp(hp, norm_mlp_w[i], w_up, w_down, final_w)
        hs = _mlp(hs, norm_mlp_w[i], w_up, w_down, final_w)

    y_prompt = hp.reshape(bp, sp, d_model)
    y_sample = hs.reshape(bs, ss, d_model)
    return (y_prompt, y_sample, jnp.stack(k_p), jnp.stack(v_p), jnp.stack(ssm_p), jnp.stack(conv_p),
            jnp.stack(k_s), jnp.stack(v_s), jnp.stack(ssm_s), jnp.stack(conv_s))
```
